```python
import math
import jax
import jax.numpy as jnp
from jax import lax
import numpy as np

D_MODEL = 1024
BATCH = 2
SEQ = 8192
DEPTH = 2
DEC_BATCH = 128
DEC_SEQ = 1
PAST_LEN = 8192
PAGE_SIZE = 128

MIX_WIDTH = D_MODEL
CONV_WIDTH = MIX_WIDTH // 4
DIFF_WIDTH = MIX_WIDTH // 4
MLA_WIDTH = MIX_WIDTH // 2
CONV_K = 3
DIFF_HEADS = 4
DIFF_V = DIFF_WIDTH // DIFF_HEADS
DIFF_D = DIFF_V // 2
DIFF_QK = DIFF_HEADS * 2 * DIFF_D
MLA_HEADS = 8
MLA_V = MLA_WIDTH // MLA_HEADS
MLA_NOPE = MLA_V
MLA_ROPE = MLA_V // 2
MLA_KV_RANK = D_MODEL // 4
MLA_Q_RANK = 3 * D_MODEL // 8
ROPE_THETA = 10000.0
IN_WIDTH = 3 * CONV_WIDTH + 2 * DIFF_QK + DIFF_WIDTH + MLA_Q_RANK + MLA_KV_RANK + MLA_ROPE
PEER_HEADS = 8
PEER_KEYS = 128
PEER_DKEY = 256
PEER_TOPK = 16
N_EXPERTS = PEER_KEYS * PEER_KEYS
NORM_EPS = 1e-6
Q_BLOCK = 128
TOK_BLOCK = 128
NEG_INF = -1e30

kernel_name = "hymba_conv_diffattn_mla_peer_step"


def _rmsnorm(x, g):
    xf = x.astype(jnp.float32)
    y = xf * lax.rsqrt(jnp.mean(xf * xf, axis=-1, keepdims=True) + NORM_EPS)
    return (y * g.astype(jnp.float32)).astype(x.dtype)


def _rope_tables(pos):
    half = MLA_ROPE // 2
    inv = ROPE_THETA ** (-jnp.arange(half, dtype=jnp.float32) * (2.0 / MLA_ROPE))
    ang = pos.astype(jnp.float32)[:, None] * inv[None, :]
    return jnp.cos(ang), jnp.sin(ang)


def _rope(x, cos, sin):
    x1, x2 = jnp.split(x.astype(jnp.float32), 2, axis=-1)
    return jnp.concatenate([x1 * cos - x2 * sin, x1 * sin + x2 * cos], axis=-1).astype(x.dtype)


def _masked_softmax(sc, qpos, kpos):
    allowed = kpos[None, :] <= qpos[:, None]
    return jax.nn.softmax(jnp.where(allowed, sc.astype(jnp.float32), NEG_INF), axis=-1)


def _weighted_sum(p, parts, eq):
    out, off = None, 0
    for part in parts:
        n = part.shape[1]
        term = jnp.einsum(eq, p[..., off:off + n].astype(part.dtype), part)
        out = term if out is None else out + term
        off += n
    return out


def _causal_blocks(fn, q_arrays, qpos):
    s = qpos.shape[0]

    def step(i):
        st = i * Q_BLOCK
        qb = tuple(lax.dynamic_slice_in_dim(a, st, Q_BLOCK, axis=1) for a in q_arrays)
        return fn(qb, lax.dynamic_slice_in_dim(qpos, st, Q_BLOCK))

    out = lax.map(step, jnp.arange(s // Q_BLOCK))
    out = jnp.moveaxis(out, 0, 1)
    return out.reshape((out.shape[0], s) + out.shape[3:])


def _short_conv(bg, cg, hx, conv_w, state):
    z = cg * hx
    zp = jnp.concatenate([state.astype(z.dtype), z], axis=1)
    s = z.shape[1]
    y = sum(conv_w[j] * zp[:, j:j + s] for j in range(CONV_K))
    return bg * y, zp[:, s:]


def _diff_attend(q, qpos, k_parts, v_parts, kpos, lam):
    sc = jnp.concatenate([jnp.einsum("bqhmd,bkhmd->mbhqk", q, k) for k in k_parts], axis=-1)
    p = _masked_softmax(sc * (DIFF_D ** -0.5), qpos, kpos)
    a = p[0] - lam * p[1]
    return _weighted_sum(a, v_parts, "bhqk,bkhv->bqhv")


def _mla_attend_prompt(qn, qpe, qpos, kn, kpe, v, kpos):
    sc = jnp.einsum("bqhn,bkhn->bhqk", qn, kn) + jnp.einsum("bqhr,bkr->bhqk", qpe, kpe)
    p = _masked_softmax(sc * ((MLA_NOPE + MLA_ROPE) ** -0.5), qpos, kpos)
    return jnp.einsum("bhqk,bkhv->bqhv", p.astype(v.dtype), v)


def _mla_attend_latent(qlat, qpe, qpos, ckv_parts, kpe_parts, kpos, w_uv):
    sc = jnp.concatenate([jnp.einsum("bqhc,bkc->bhqk", qlat, c) + jnp.einsum("bqhr,bkr->bhqk", qpe, r)
                          for c, r in zip(ckv_parts, kpe_parts)], axis=-1)
    p = _masked_softmax(sc * ((MLA_NOPE + MLA_ROPE) ** -0.5), qpos, kpos)
    olat = _weighted_sum(p, ckv_parts, "bhqk,bkc->bqhc")
    return jnp.einsum("bqhc,chv->bqhv", olat, w_uv)


def _peer(x, w_q, sub_keys, u, v):
    t = x.shape[0]
    nb = -(-t // TOK_BLOCK)
    xb = jnp.pad(x, ((0, nb * TOK_BLOCK - t), (0, 0))).reshape(nb, TOK_BLOCK, D_MODEL)

    def block(xt):
        q = (xt @ w_q).reshape(TOK_BLOCK, PEER_HEADS, 2, PEER_DKEY // 2)
        s = jnp.einsum("thpe,hpne->thpn", q, sub_keys).astype(jnp.float32)
        sv, si = lax.top_k(s, PEER_TOPK)
        cand_s = (sv[:, :, 0, :, None] + sv[:, :, 1, None, :]).reshape(TOK_BLOCK, PEER_HEADS, PEER_TOPK * PEER_TOPK)
        cand_i = (si[:, :, 0, :, None] * PEER_KEYS + si[:, :, 1, None, :]).reshape(TOK_BLOCK, PEER_HEADS, PEER_TOPK * PEER_TOPK)
        top_s, top_j = lax.top_k(cand_s, PEER_TOPK)
        e = jnp.take_along_axis(cand_i, top_j, axis=-1)
        g = jax.nn.softmax(top_s, axis=-1)
        act = jax.nn.gelu(jnp.einsum("td,thkd->thk", xt, u[e]), approximate=False)
        return jnp.einsum("thk,thkd->td", (g * act).astype(xt.dtype), v[e])

    return lax.map(block, xb).reshape(nb * TOK_BLOCK, D_MODEL)[:t]


def _gather_pages(pool, layer, page_table):
    g = pool[layer, page_table]
    return g.reshape(g.shape[0], -1, g.shape[-1])


def _layer(x, pos, conv_state, past, w, lam_init):
    b, s, _ = x.shape
    h = _rmsnorm(x, w["ln1"])
    proj = h @ w["w_in"]
    sizes = (CONV_WIDTH, CONV_WIDTH, CONV_WIDTH, DIFF_QK, DIFF_QK, DIFF_WIDTH, MLA_Q_RANK, MLA_KV_RANK, MLA_ROPE)
    bg, cg, hx, dq, dk, dv, cq, ckv, kpe = jnp.split(proj, np.cumsum(sizes)[:-1].tolist(), axis=-1)

    if conv_state is None:
        conv_state = jnp.zeros((b, CONV_K - 1, CONV_WIDTH), x.dtype)
    conv_out, new_conv = _short_conv(bg, cg, hx, w["conv_w"], conv_state)

    lp = w["diff_lambda"].astype(jnp.float32)
    lam = jnp.exp(jnp.sum(lp[0] * lp[1])) - jnp.exp(jnp.sum(lp[2] * lp[3])) + lam_init
    dq4 = dq.reshape(b, s, DIFF_HEADS, 2, DIFF_D)
    dk4 = dk.reshape(b, s, DIFF_HEADS, 2, DIFF_D)
    dv4 = dv.reshape(b, s, DIFF_HEADS, DIFF_V)

    q = jnp.einsum("bsr,rhe->bshe", _rmsnorm(cq, w["q_norm"]), w["w_uq"])
    qn, qpe = q[..., :MLA_NOPE], q[..., MLA_NOPE:]
    cos, sin = _rope_tables(pos)
    qpe = _rope(qpe, cos[:, None], sin[:, None])
    ckv = _rmsnorm(ckv, w["kv_norm"])
    kpe = _rope(kpe, cos, sin)

    if past is None:
        kn = jnp.einsum("bsc,chn->bshn", ckv, w["w_uk"])
        vm = jnp.einsum("bsc,chv->bshv", ckv, w["w_uv"])
        diff_out = _causal_blocks(
            lambda qs, qp: _diff_attend(qs[0], qp, [dk4], [dv4], pos, lam), (dq4,), pos)
        mla_out = _causal_blocks(
            lambda qs, qp: _mla_attend_prompt(qs[0], qs[1], qp, kn, kpe, vm, pos), (qn, qpe), pos)
    else:
        k_past, v_past, ckv_past, kpe_past = past
        kpos = jnp.concatenate([jnp.arange(k_past.shape[1]), pos])
        diff_out = _diff_attend(dq4, pos,
                                [k_past.reshape(b, -1, DIFF_HEADS, 2, DIFF_D), dk4],
                                [v_past.reshape(b, -1, DIFF_HEADS, DIFF_V), dv4], kpos, lam)
        qlat = jnp.einsum("bqhn,chn->bqhc", qn, w["w_uk"])
        mla_out = _mla_attend_latent(qlat, qpe, pos, [ckv_past, ckv], [kpe_past, kpe], kpos, w["w_uv"])

    diff_out = _rmsnorm(diff_out, w["diff_norm"]) * (1.0 - lam_init)
    mix = jnp.concatenate([conv_out, diff_out.reshape(b, s, DIFF_WIDTH), mla_out.reshape(b, s, MLA_WIDTH)], axis=-1)
    x = x + mix @ w["w_out"]
    h2 = _rmsnorm(x, w["ln2"])
    x = x + _peer(h2.reshape(b * s, D_MODEL), w["peer_w_q"], w["peer_keys"], w["peer_u"], w["peer_v"]).reshape(b, s, D_MODEL)
    return x, (new_conv, dk, dv, ckv, kpe)


def setup_inputs(seed: int = 0) -> dict:
    key = jax.random.key(seed)
    ks = jax.random.split(key, 32)
    f32 = jnp.float32
    n_pages = PAST_LEN // PAGE_SIZE
    n_used = DEC_BATCH * n_pages
    n_pool = n_used + max(n_used // 4, 1)

    def nrm(k, shape, scale):
        return jax.random.normal(k, shape, f32) * scale

    def gain(k, shape):
        return 1.0 + nrm(k, shape, 0.05)

    page_table = jax.random.permutation(ks[0], n_pool)[:n_used].reshape(DEC_BATCH, n_pages).astype(jnp.int32)
    return {
        "x_prompt": nrm(ks[1], (BATCH, SEQ, D_MODEL), 1.0),
        "x_sample": nrm(ks[2], (DEC_BATCH, DEC_SEQ, D_MODEL), 1.0),
        "state_conv": nrm(ks[3], (DEPTH, DEC_BATCH, CONV_K - 1, CONV_WIDTH), 1.0),
        "cache_diff_k": nrm(ks[4], (DEPTH, n_pool, PAGE_SIZE, DIFF_QK), 1.0),
        "cache_diff_v": nrm(ks[5], (DEPTH, n_pool, PAGE_SIZE, DIFF_WIDTH), 1.0),
        "cache_mla_ckv": nrm(ks[6], (DEPTH, n_pool, PAGE_SIZE, MLA_KV_RANK), 1.0),
        "cache_mla_kpe": nrm(ks[7], (DEPTH, n_pool, PAGE_SIZE, MLA_ROPE), 1.0),
        "page_table": page_table,
        "ln1_g": gain(ks[8], (DEPTH, D_MODEL)),
        "ln2_g": gain(ks[9], (DEPTH, D_MODEL)),
        "w_in": nrm(ks[10], (DEPTH, D_MODEL, IN_WIDTH), D_MODEL ** -0.5),
        "conv_w": nrm(ks[11], (DEPTH, CONV_K, CONV_WIDTH), CONV_K ** -0.5),
        "diff_lambda": nrm(ks[12], (DEPTH, 4, DIFF_D), 0.1),
        "diff_norm_g": gain(ks[13], (DEPTH, DIFF_V)),
        "mla_q_norm_g": gain(ks[14], (DEPTH, MLA_Q_RANK)),
        "mla_kv_norm_g": gain(ks[15], (DEPTH, MLA_KV_RANK)),
        "mla_w_uq": nrm(ks[16], (DEPTH, MLA_Q_RANK, MLA_HEADS, MLA_NOPE + MLA_ROPE), MLA_Q_RANK ** -0.5),
        "mla_w_uk": nrm(ks[17], (DEPTH, MLA_KV_RANK, MLA_HEADS, MLA_NOPE), MLA_KV_RANK ** -0.5),
        "mla_w_uv": nrm(ks[18], (DEPTH, MLA_KV_RANK, MLA_HEADS, MLA_V), MLA_KV_RANK ** -0.5),
        "w_out": nrm(ks[19], (DEPTH, MIX_WIDTH, D_MODEL), MIX_WIDTH ** -0.5),
        "peer_w_q": nrm(ks[20], (DEPTH, D_MODEL, PEER_HEADS * PEER_DKEY), D_MODEL ** -0.5),
        "peer_sub_keys": nrm(ks[21], (DEPTH, PEER_HEADS, 2, PEER_KEYS, PEER_DKEY // 2), (PEER_DKEY // 2) ** -0.5),
        "peer_u": nrm(ks[22], (DEPTH, N_EXPERTS, D_MODEL), D_MODEL ** -0.5),
        "peer_v": nrm(ks[23], (DEPTH, N_EXPERTS, D_MODEL), PEER_HEADS ** -0.5),
        "final_norm_g": gain(ks[24], (D_MODEL,)),
    }


def reference(x_prompt, x_sample, state_conv, cache_diff_k, cache_diff_v, cache_mla_ckv, cache_mla_kpe,
              page_table, ln1_g, ln2_g, w_in, conv_w, diff_lambda, diff_norm_g, mla_q_norm_g,
              mla_kv_norm_g, mla_w_uq, mla_w_uk, mla_w_uv, w_out, peer_w_q, peer_sub_keys, peer_u,
              peer_v, final_norm_g):
    pos_p = jnp.arange(x_prompt.shape[1])
    pos_s = PAST_LEN + jnp.arange(x_sample.shape[1])
    xp, xs = x_prompt, x_sample
    rows_p = ([], [], [], [], [])
    rows_s = ([], [], [], [], [])
    for l in range(DEPTH):
        w = {
            "ln1": ln1_g[l], "ln2": ln2_g[l], "w_in": w_in[l], "conv_w": conv_w[l],
            "diff_lambda": diff_lambda[l], "diff_norm": diff_norm_g[l],
            "q_norm": mla_q_norm_g[l], "kv_norm": mla_kv_norm_g[l], "w_uq": mla_w_uq[l],
            "w_uk": mla_w_uk[l], "w_uv": mla_w_uv[l], "w_out": w_out[l],
            "peer_w_q": peer_w_q[l], "peer_keys": peer_sub_keys[l], "peer_u": peer_u[l], "peer_v": peer_v[l],
        }
        lam_init = 0.8 - 0.6 * math.exp(-0.3 * l)
        xp, new_p = _layer(xp, pos_p, None, None, w, lam_init)
        past = (_gather_pages(cache_diff_k, l, page_table), _gather_pages(cache_diff_v, l, page_table),
                _gather_pages(cache_mla_ckv, l, page_table), _gather_pages(cache_mla_kpe, l, page_table))
        xs, new_s = _layer(xs, pos_s, state_conv[l], past, w, lam_init)
        for lst, r in zip(rows_p, new_p):
            lst.append(r)
        for lst, r in zip(rows_s, new_s):
            lst.append(r)
    y_prompt = _rmsnorm(xp, final_norm_g)
    y_sample = _rmsnorm(xs, final_norm_g)
    conv_p, dk_p, dv_p, ckv_p, kpe_p = (jnp.stack(r) for r in rows_p)
    conv_s, dk_s, dv_s, ckv_s, kpe_s = (jnp.stack(r) for r in rows_s)
    return (y_prompt, y_sample, conv_p, dk_p, dv_p, ckv_p, kpe_p, conv_s, dk_s, dv_s, ckv_s, kpe_s)
```

```python
import functools
import math

import jax
import jax.numpy as jnp
from jax import lax
from jax.experimental import pallas as pl
from jax.experimental.pallas import tpu as pltpu

F32 = jnp.float32
BF16 = jnp.bfloat16
I32 = jnp.int32

D_MODEL = 1024
CONV_W = 256
CONV_K = 3
DIFF_HEADS = 4
DIFF_D = 32
DIFF_V = 64
DIFF_QK = 256
DIFF_W = 256
MLA_HEADS = 8
MLA_NOPE = 64
MLA_ROPE = 32
MLA_V = 64
MLA_KV = 256
MLA_Q = 384
MLA_W = 512
MLA_HEAD_PAD = 128
PEER_HEADS = 8
PEER_KEYS = 128
PEER_DKEY = 256
PEER_TOPK = 16
N_EXPERTS = PEER_KEYS * PEER_KEYS
PEER_SEL = PEER_HEADS * PEER_TOPK
ROPE_THETA = 10000.0
NORM_EPS = 1e-6
NEG_INF = -1e30
PAGE = 128
IN_W = 2208
IN_W_EXT = 2304
LANES = 128
SUBLANES = 8
VMEM_LIMIT = 56 * 1024 * 1024

DIFF_SCALE = DIFF_D ** -0.5
MLA_SCALE = (MLA_NOPE + MLA_ROPE) ** -0.5

_NT = (((1,), (1,)), ((), ()))


def _cparams(n_axes):
    return pltpu.CompilerParams(dimension_semantics=("arbitrary",) * n_axes,
                                vmem_limit_bytes=VMEM_LIMIT)


def _rms(x, g):
    ms = jnp.mean(x * x, axis=-1, keepdims=True)
    return x * lax.rsqrt(ms + NORM_EPS) * g


def _dot(a, b):
    return jnp.dot(a, b, preferred_element_type=F32)


def _dot_nt(a, b):
    return lax.dot_general(a, b, _NT, preferred_element_type=F32)


def _full_spec(shape):
    nd = len(shape)
    return pl.BlockSpec(shape, lambda *_: (0,) * nd)


def _proj_body(decode, tb, *refs):
    (x_ref, g1_ref, win_ref, cw_ref, qg_ref, kvg_ref, wq_ref, wqr_ref, wk_ref, wv_ref,
     cq_ref, ck_ref, sn_ref) = refs[:13]
    rest = refs[13:]
    if decode:
        s0_ref, s1_ref = rest[:2]
        rest = rest[2:]
    (y_ref, dqb_ref, dk_ref, dkb_ref, dv_ref, dvb_ref, ckv_ref, kpe_ref,
     qm_ref, km_ref, vm_ref, zo_ref) = rest[:12]

    x = x_ref[0]
    h = _rms(x, g1_ref[...]).astype(BF16)

    def proj(a, b):
        return _dot(h, win_ref[:, a:b])

    bg = proj(0, 256)
    z = proj(256, 512) * proj(512, 768)
    w0, w1, w2 = cw_ref[0:1, :], cw_ref[1:2, :], cw_ref[2:3, :]
    if decode:
        y = bg * (w0 * s0_ref[0] + w1 * s1_ref[0] + w2 * z)
        zo_ref[0] = z
    else:
        zs_ref = rest[12]
        j = pl.program_id(1)

        @pl.when(j == 0)
        def _():
            zs_ref[0:SUBLANES, :] = jnp.zeros((SUBLANES, CONV_W), F32)

        zs_ref[SUBLANES:SUBLANES + tb, :] = z
        zm1 = zs_ref[SUBLANES - 1:SUBLANES - 1 + tb, :]
        zm2 = zs_ref[SUBLANES - 2:SUBLANES - 2 + tb, :]
        y = bg * (w0 * zm2 + w1 * zm1 + w2 * z)
        zs_ref[0:SUBLANES, :] = z[tb - SUBLANES:tb, :]
        zo_ref[0] = z[tb - 2:tb, :]
    y_ref[0] = y.astype(BF16)

    dqb_ref[0] = (proj(768, 1024) * DIFF_SCALE).astype(BF16)
    dk = proj(1024, 1280)
    dk_ref[0] = dk
    dkb_ref[0] = dk.astype(BF16)
    dv = proj(1280, 1536)
    dv_ref[0] = dv
    dvb_ref[0] = dv.astype(BF16)

    cqn = _rms(proj(1536, 1920), qg_ref[...]).astype(BF16)
    cq8 = jnp.concatenate([cq_ref[0]] * MLA_HEADS, axis=1)
    sn8 = jnp.concatenate([sn_ref[0]] * MLA_HEADS, axis=1)
    qm = (_dot(cqn, wq_ref[...]) * cq8 + _dot(cqn, wqr_ref[...]) * sn8) * MLA_SCALE
    qm_ref[0] = qm.astype(BF16)

    ckvn = _rms(proj(1920, 2176), kvg_ref[...])
    ckv_ref[0] = ckvn
    ckb = ckvn.astype(BF16)
    tl = proj(2176, 2304)
    kr = tl * ck_ref[0] + pltpu.roll(tl, LANES - MLA_ROPE, 1) * sn_ref[0]
    kpe_ref[0] = kr[:, MLA_NOPE:MLA_NOPE + MLA_ROPE]
    km = _dot(ckb, wk_ref[...]) + jnp.concatenate([kr] * MLA_HEADS, axis=1)
    km_ref[0] = km.astype(BF16)
    vm_ref[0] = _dot(ckb, wv_ref[...]).astype(BF16)


def _proj_call(x3, lw, tabs, state=None):
    b, s, _ = x3.shape
    decode = state is not None
    tb = min(s, 512)
    assert s % tb == 0
    grid = (b, s // tb)

    def tok(w):
        return pl.BlockSpec((1, tb, w), lambda i, j: (i, j, 0))

    weights = [lw["g1"], lw["w_in"], lw["conv_w"], lw["qg"], lw["kvg"], lw["wq"], lw["wqr"], lw["wk"], lw["wv"]]
    tab = pl.BlockSpec((1, tb, LANES), lambda i, j: (0, j, 0))
    in_specs = [tok(D_MODEL)] + [_full_spec(w.shape) for w in weights] + [tab] * 3
    args = [x3] + weights + list(tabs)
    if decode:
        in_specs += [tok(CONV_W)] * 2
        args += list(state)

    def sds(w, dt):
        return jax.ShapeDtypeStruct((b, s, w), dt)

    out_shape = [sds(CONV_W, BF16), sds(DIFF_QK, BF16), sds(DIFF_QK, F32), sds(DIFF_QK, BF16),
                 sds(DIFF_W, F32), sds(DIFF_W, BF16), sds(MLA_KV, F32), sds(MLA_ROPE, F32),
                 sds(MLA_HEADS * MLA_HEAD_PAD, BF16), sds(MLA_HEADS * MLA_HEAD_PAD, BF16), sds(MLA_W, BF16)]
    out_specs = [tok(CONV_W), tok(DIFF_QK), tok(DIFF_QK), tok(DIFF_QK), tok(DIFF_W), tok(DIFF_W),
                 tok(MLA_KV), tok(MLA_ROPE), tok(MLA_HEADS * MLA_HEAD_PAD), tok(MLA_HEADS * MLA_HEAD_PAD),
                 tok(MLA_W)]
    scratch = []
    if decode:
        out_shape.append(sds(CONV_W, F32))
        out_specs.append(tok(CONV_W))
    else:
        out_shape.append(jax.ShapeDtypeStruct((b, CONV_K - 1, CONV_W), F32))
        out_specs.append(pl.BlockSpec((1, CONV_K - 1, CONV_W), lambda i, j: (i, 0, 0)))
        scratch.append(pltpu.VMEM((tb + SUBLANES, CONV_W), F32))
    return pl.pallas_call(
        functools.partial(_proj_body, decode, tb),
        grid=grid, in_specs=in_specs, out_specs=out_specs, out_shape=out_shape,
        scratch_shapes=scratch, compiler_params=_cparams(2), name="proj",
    )(*args)


def _online_update(s, v, m_ref, l_ref, acc_ref, g):
    m_prev = m_ref[g]
    m_new = jnp.maximum(m_prev, jnp.max(s, axis=-1, keepdims=True))
    alpha = jnp.exp(m_prev - m_new)
    p = jnp.exp(s - m_new)
    l_ref[g] = alpha * l_ref[g] + jnp.sum(p, axis=-1, keepdims=True)
    acc_ref[g] = alpha * acc_ref[g] + _dot(p.astype(BF16), v)
    m_ref[g] = m_new


def _init_state(m_ref, l_ref, acc_ref):
    m_ref[...] = jnp.full(m_ref.shape, NEG_INF, F32)
    l_ref[...] = jnp.zeros(l_ref.shape, F32)
    acc_ref[...] = jnp.zeros(acc_ref.shape, F32)


def _causal_mask(tq):
    row = lax.broadcasted_iota(I32, (tq, tq), 0)
    col = lax.broadcasted_iota(I32, (tq, tq), 1)
    return col <= row


def _diff_attn_body(tq, lam_ref, q_ref, k_ref, v_ref, g_ref, o_ref, m_ref, l_ref, acc_ref):
    i = pl.program_id(2)
    q = q_ref[0]
    lane = lax.broadcasted_iota(I32, (1, LANES), 1)
    seg = lane >> 5
    qs = [jnp.where(seg == g, q, jnp.zeros_like(q)) for g in range(4)]
    _init_state(m_ref, l_ref, acc_ref)

    def step(j, masked):
        off = pl.multiple_of(j * tq, tq)
        k = k_ref[0, pl.ds(off, tq), :]
        v = v_ref[0, pl.ds(off, tq), :]
        for g in range(4):
            s = _dot_nt(qs[g], k)
            if masked:
                s = jnp.where(_causal_mask(tq), s, NEG_INF)
            _online_update(s, v, m_ref, l_ref, acc_ref, g)

    def body(j, c):
        step(j, False)
        return c

    lax.fori_loop(0, i, body, 0)
    step(i, True)

    lam = lam_ref[0]
    o0 = acc_ref[0] / l_ref[0] - lam * (acc_ref[1] / l_ref[1])
    o1 = acc_ref[2] / l_ref[2] - lam * (acc_ref[3] / l_ref[3])
    first = lane < DIFF_V
    o = jnp.where(first, o0, o1)
    sq = o * o
    ss0 = jnp.sum(jnp.where(first, sq, 0.0), axis=-1, keepdims=True)
    ss1 = jnp.sum(jnp.where(first, 0.0, sq), axis=-1, keepdims=True)
    ms = jnp.where(first, ss0, ss1) * (1.0 / DIFF_V)
    o_ref[0] = (o * lax.rsqrt(ms + NORM_EPS) * g_ref[...]).astype(BF16)


def _diff_attn_call(dq, dk, dv, lam, gain, tq):
    b, s, _ = dq.shape
    n_pairs = DIFF_QK // LANES
    grid = (b, n_pairs, s // tq)
    return pl.pallas_call(
        functools.partial(_diff_attn_body, tq),
        grid=grid,
        in_specs=[pl.BlockSpec(memory_space=pltpu.SMEM),
                  pl.BlockSpec((1, tq, LANES), lambda bi, hp, i: (bi, i, hp)),
                  pl.BlockSpec((1, s, LANES), lambda bi, hp, i: (bi, 0, hp)),
                  pl.BlockSpec((1, s, LANES), lambda bi, hp, i: (bi, 0, hp)),
                  pl.BlockSpec((1, LANES), lambda bi, hp, i: (0, hp))],
        out_specs=pl.BlockSpec((1, tq, LANES), lambda bi, hp, i: (bi, i, hp)),
        out_shape=jax.ShapeDtypeStruct((b, s, DIFF_W), BF16),
        scratch_shapes=[pltpu.VMEM((4, tq, 1), F32), pltpu.VMEM((4, tq, 1), F32),
                        pltpu.VMEM((4, tq, LANES), F32)],
        compiler_params=_cparams(3), name="diff_attn",
    )(lam, dq, dk, dv, gain)


def _mla_attn_body(tq, q_ref, k_ref, v_ref, o_ref, m_ref, l_ref, acc_ref):
    i = pl.program_id(2)
    q = q_ref[0]
    _init_state(m_ref, l_ref, acc_ref)

    def step(j, masked):
        off = pl.multiple_of(j * tq, tq)
        k = k_ref[0, pl.ds(off, tq), :]
        v = v_ref[0, pl.ds(off, tq), :]
        for hh in range(2):
            sl = slice(hh * MLA_HEAD_PAD, (hh + 1) * MLA_HEAD_PAD)
            s = _dot_nt(q[:, sl], k[:, sl])
            if masked:
                s = jnp.where(_causal_mask(tq), s, NEG_INF)
            _online_update(s, v, m_ref, l_ref, acc_ref, hh)

    def body(j, c):
        step(j, False)
        return c

    lax.fori_loop(0, i, body, 0)
    step(i, True)
    lane = lax.broadcasted_iota(I32, (1, LANES), 1)
    o = jnp.where(lane < MLA_V, acc_ref[0] / l_ref[0], acc_ref[1] / l_ref[1])
    o_ref[0] = o.astype(BF16)


def _mla_attn_call(qm, km, vm, tq):
    b, s, _ = qm.shape
    n_pairs = MLA_HEADS // 2
    grid = (b, n_pairs, s // tq)
    qw = 2 * MLA_HEAD_PAD
    return pl.pallas_call(
        functools.partial(_mla_attn_body, tq),
        grid=grid,
        in_specs=[pl.BlockSpec((1, tq, qw), lambda bi, hp, i: (bi, i, hp)),
                  pl.BlockSpec((1, s, qw), lambda bi, hp, i: (bi, 0, hp)),
                  pl.BlockSpec((1, s, LANES), lambda bi, hp, i: (bi, 0, hp))],
        out_specs=pl.BlockSpec((1, tq, LANES), lambda bi, hp, i: (bi, i, hp)),
        out_shape=jax.ShapeDtypeStruct((b, s, MLA_W), BF16),
        scratch_shapes=[pltpu.VMEM((2, tq, 1), F32), pltpu.VMEM((2, tq, 1), F32),
                        pltpu.VMEM((2, tq, LANES), F32)],
        compiler_params=_cparams(3), name="mla_attn",
    )(qm, km, vm)


def _decode_body(pps, *refs):
    pt_ref = refs[0]
    del pt_ref
    (dq_ref, qm_ref, dks_ref, dvs_ref, ckvs_ref, kpes_ref, wk_ref, sel_ref, wuv_ref,
     gain_ref, lam_ref) = refs[1:12]
    pages = refs[12:12 + 4 * pps]
    kd_refs, vd_refs = pages[:pps], pages[pps:2 * pps]
    ckv_refs, kpe_refs = pages[2 * pps:3 * pps], pages[3 * pps:]
    do_ref, mo_ref = refs[12 + 4 * pps:14 + 4 * pps]
    (qbd_ref, qlat_ref, qpe_ref, md_ref, ld_ref, accd_ref, mm_ref, lm_ref, accm_ref) = refs[14 + 4 * pps:]

    p = pl.program_id(1)
    n_steps = pl.num_programs(1)
    n_rows = SUBLANES

    @pl.when(p == 0)
    def _():
        sub = lax.broadcasted_iota(I32, (n_rows, DIFF_QK), 0)
        lane = lax.broadcasted_iota(I32, (n_rows, DIFF_QK), 1)
        row = jnp.broadcast_to(dq_ref[0].astype(F32), (n_rows, DIFF_QK))
        qbd = jnp.where((lane >> 5) == sub, row, 0.0)
        qbd_ref[...] = qbd.astype(BF16)
        md_ref[...] = jnp.sum(qbd * dks_ref[0], axis=1, keepdims=True)
        ld_ref[...] = jnp.ones((n_rows, 1), F32)
        accd_ref[...] = jnp.broadcast_to(dvs_ref[0], (n_rows, DIFF_W))

        wq = MLA_HEADS * MLA_HEAD_PAD
        sub = lax.broadcasted_iota(I32, (n_rows, wq), 0)
        lane = lax.broadcasted_iota(I32, (n_rows, wq), 1)
        row = jnp.broadcast_to(qm_ref[0].astype(F32), (n_rows, wq))
        qf = jnp.where((lane >> 7) == sub, row, 0.0).astype(BF16)
        qlat = _dot_nt(qf, wk_ref[...]).astype(BF16)
        qpe = _dot(qf, sel_ref[...]).astype(BF16)
        qlat_ref[...] = qlat
        qpe_ref[...] = qpe
        mm_ref[...] = (jnp.sum(qlat.astype(F32) * ckvs_ref[0], axis=1, keepdims=True)
                       + jnp.sum(qpe.astype(F32) * kpes_ref[0], axis=1, keepdims=True))
        lm_ref[...] = jnp.ones((n_rows, 1), F32)
        accm_ref[...] = jnp.broadcast_to(ckvs_ref[0], (n_rows, MLA_KV))

    def cat(rs):
        return jnp.concatenate([r[0, 0] for r in rs], axis=0).astype(BF16)

    def update(s, v, m_ref, l_ref, acc_ref):
        m_prev = m_ref[...]
        m_new = jnp.maximum(m_prev, jnp.max(s, axis=-1, keepdims=True))
        alpha = jnp.exp(m_prev - m_new)
        pr = jnp.exp(s - m_new)
        l_ref[...] = alpha * l_ref[...] + jnp.sum(pr, axis=-1, keepdims=True)
        acc_ref[...] = alpha * acc_ref[...] + _dot(pr.astype(BF16), v)
        m_ref[...] = m_new

    update(_dot_nt(qbd_ref[...], cat(kd_refs)), cat(vd_refs), md_ref, ld_ref, accd_ref)
    c = cat(ckv_refs)
    update(_dot_nt(qlat_ref[...], c) + _dot_nt(qpe_ref[...], cat(kpe_refs)), c, mm_ref, lm_ref, accm_ref)

    @pl.when(p == n_steps - 1)
    def _():
        lam = lam_ref[...]
        sub = lax.broadcasted_iota(I32, (n_rows, DIFF_W), 0)
        lane = lax.broadcasted_iota(I32, (n_rows, DIFF_W), 1)
        coef = jnp.where((sub & 1) == 0, 1.0, -lam)
        o2 = accd_ref[...] / ld_ref[...] * coef
        orow = jnp.sum(jnp.where((lane >> 6) == (sub >> 1), o2, 0.0), axis=0, keepdims=True)
        lane1 = lane[0:1, :] >> 6
        sq = orow * orow
        ms = jnp.zeros_like(orow)
        for g in range(DIFF_HEADS):
            ssg = jnp.sum(jnp.where(lane1 == g, sq, 0.0), axis=-1, keepdims=True)
            ms = jnp.where(lane1 == g, ssg, ms)
        ms = ms * (1.0 / DIFF_V)
        do_ref[0] = (orow * lax.rsqrt(ms + NORM_EPS) * gain_ref[...]).astype(BF16)

        olat = (accm_ref[...] / lm_ref[...]).astype(BF16)
        r = _dot(olat, wuv_ref[...])
        sub = lax.broadcasted_iota(I32, (n_rows, MLA_W), 0)
        lane = lax.broadcasted_iota(I32, (n_rows, MLA_W), 1)
        mo_ref[0] = jnp.sum(jnp.where((lane >> 6) == sub, r, 0.0), axis=0, keepdims=True).astype(BF16)


def _decode_call(layer, page_table, caches, dq, qm, dks, dvs, ckvs, kpes, lw, pps):
    ns, n_pages = page_table.shape
    assert n_pages % pps == 0
    n_steps = n_pages // pps
    cache_k, cache_v, cache_ckv, cache_kpe = caches

    def tok(w):
        return pl.BlockSpec((1, 1, w), lambda bi, p, pt: (bi, 0, 0))

    def full(shape):
        nd = len(shape)
        return pl.BlockSpec(shape, lambda bi, p, pt: (0,) * nd)

    def page(w, r):
        return pl.BlockSpec((1, 1, PAGE, w),
                            lambda bi, p, pt: (layer, pt[bi * n_pages + p * pps + r], 0, 0))

    weights = [lw["wk"], lw["sel"], lw["wv"], lw["diff_gain"], lw["lam11"]]
    in_specs = ([tok(DIFF_QK), tok(MLA_HEADS * MLA_HEAD_PAD), tok(DIFF_QK), tok(DIFF_W), tok(MLA_KV), tok(MLA_ROPE)]
                + [full(w.shape) for w in weights]
                + [page(DIFF_QK, r) for r in range(pps)] + [page(DIFF_W, r) for r in range(pps)]
                + [page(MLA_KV, r) for r in range(pps)] + [page(MLA_ROPE, r) for r in range(pps)])
    args = ([dq, qm, dks, dvs, ckvs, kpes] + weights
            + [cache_k] * pps + [cache_v] * pps + [cache_ckv] * pps + [cache_kpe] * pps)
    grid_spec = pltpu.PrefetchScalarGridSpec(
        num_scalar_prefetch=1, grid=(ns, n_steps), in_specs=in_specs,
        out_specs=[tok(DIFF_W), tok(MLA_W)],
        scratch_shapes=[pltpu.VMEM((SUBLANES, DIFF_QK), BF16), pltpu.VMEM((SUBLANES, MLA_KV), BF16),
                        pltpu.VMEM((SUBLANES, MLA_ROPE), BF16),
                        pltpu.VMEM((SUBLANES, 1), F32), pltpu.VMEM((SUBLANES, 1), F32),
                        pltpu.VMEM((SUBLANES, DIFF_W), F32),
                        pltpu.VMEM((SUBLANES, 1), F32), pltpu.VMEM((SUBLANES, 1), F32),
                        pltpu.VMEM((SUBLANES, MLA_KV), F32)])
    return pl.pallas_call(
        functools.partial(_decode_body, pps),
        grid_spec=grid_spec,
        out_shape=[jax.ShapeDtypeStruct((ns, 1, DIFF_W), BF16), jax.ShapeDtypeStruct((ns, 1, MLA_W), BF16)],
        compiler_params=_cparams(2), name="decode_attn",
    )(page_table.reshape(-1), *args)


def _post_body(x_ref, y_ref, d_ref, m_ref, wo_ref, g2_ref, wqh_ref, wql_ref, x1_ref, h2_ref, q_ref):
    x1 = (x_ref[...] + _dot(y_ref[...], wo_ref[0:CONV_W, :])
          + _dot(d_ref[...], wo_ref[CONV_W:CONV_W + DIFF_W, :])
          + _dot(m_ref[...], wo_ref[CONV_W + DIFF_W:, :]))
    x1_ref[...] = x1
    h2 = _rms(x1, g2_ref[...])
    hb = h2.astype(BF16)
    hl = (h2 - hb.astype(F32)).astype(BF16)
    q_ref[...] = _dot(hb, wqh_ref[...]) + _dot(hb, wql_ref[...]) + _dot(hl, wqh_ref[...])
    h2_ref[...] = hb


def _post_call(x, y, d, m, lw):
    t = x.shape[0]
    tb = min(t, 256)
    qw = PEER_HEADS * PEER_DKEY

    def tok(w):
        return pl.BlockSpec((tb, w), lambda i: (i, 0))

    weights = [lw["w_out"], lw["g2"], lw["wq_hi"], lw["wq_lo"]]
    return pl.pallas_call(
        _post_body, grid=(t // tb,),
        in_specs=[tok(D_MODEL), tok(CONV_W), tok(DIFF_W), tok(MLA_W)] + [_full_spec(w.shape) for w in weights],
        out_specs=[tok(D_MODEL), tok(D_MODEL), tok(qw)],
        out_shape=[jax.ShapeDtypeStruct((t, D_MODEL), F32), jax.ShapeDtypeStruct((t, D_MODEL), BF16),
                   jax.ShapeDtypeStruct((t, qw), F32)],
        compiler_params=_cparams(1), name="post_attn",
    )(x, y, d, m, *weights)


_CAND_ROWS = PEER_TOPK + SUBLANES * (PEER_TOPK - 1)


def _cand_tables():
    flat, pen = [], []
    for a in range(PEER_TOPK):
        nb = PEER_TOPK if a == 0 else SUBLANES
        for b_ in range(nb):
            flat.append(a * PEER_TOPK + b_)
            pen.append(0.0 if (a + 1) * (b_ + 1) <= PEER_TOPK else -math.inf)
    return (jnp.asarray(flat, I32).reshape(-1, 1), jnp.asarray(pen, F32).reshape(-1, 1))


def _route_body(tb, q_ref, khi_ref, klo_ref, flat_ref, pen_ref, io_ref, jo_ref, go_ref,
                sv_ref, si_ref, i_scr, j_scr, g_scr):
    h = pl.program_id(1)
    q = q_ref[...]
    qh = q.astype(BF16)
    ql = (q - qh.astype(F32)).astype(BF16)
    n_iota = lax.broadcasted_iota(I32, (PEER_KEYS, tb), 0)

    for p in range(2):
        sl = slice(p * PEER_KEYS, (p + 1) * PEER_KEYS)
        kh, kl = khi_ref[0, p], klo_ref[0, p]
        s0 = _dot_nt(kh, qh[:, sl]) + _dot_nt(kh, ql[:, sl]) + _dot_nt(kl, qh[:, sl])

        def it(k, s, p=p):
            m = jnp.max(s, axis=0, keepdims=True)
            idx = jnp.min(jnp.where(s == m, n_iota, PEER_KEYS), axis=0, keepdims=True)
            sv_ref[p, pl.ds(k, 1), :] = m
            si_ref[p, pl.ds(k, 1), :] = idx
            return jnp.where(n_iota == idx, -jnp.inf, s)

        lax.fori_loop(0, PEER_TOPK, it, s0)

    sv0, sv1 = sv_ref[0], sv_ref[1]
    si0, si1 = si_ref[0], si_ref[1]
    parts = [sv0[0:1, :] + sv1]
    for a in range(1, PEER_TOPK):
        parts.append(sv0[a:a + 1, :] + sv1[0:SUBLANES, :])
    cand0 = jnp.concatenate(parts, axis=0) + pen_ref[...]
    flat = jnp.broadcast_to(flat_ref[...], (_CAND_ROWS, tb))
    k_iota = lax.broadcasted_iota(I32, (PEER_TOPK, tb), 0)
    base = pl.multiple_of(h * PEER_TOPK, PEER_TOPK)

    def it2(k, cand):
        m = jnp.max(cand, axis=0, keepdims=True)
        fl = jnp.min(jnp.where(cand == m, flat, PEER_TOPK * PEER_TOPK), axis=0, keepdims=True)
        a = fl >> 4
        b_ = fl & (PEER_TOPK - 1)
        g_scr[pl.ds(base + k, 1), :] = m
        i_scr[pl.ds(base + k, 1), :] = jnp.sum(jnp.where(k_iota == a, si0, 0), axis=0, keepdims=True)
        j_scr[pl.ds(base + k, 1), :] = jnp.sum(jnp.where(k_iota == b_, si1, 0), axis=0, keepdims=True)
        return jnp.where(flat == fl, -jnp.inf, cand)

    lax.fori_loop(0, PEER_TOPK, it2, cand0)
    ts = g_scr[pl.ds(base, PEER_TOPK), :]
    e = jnp.exp(ts - ts[0:1, :])
    g_scr[pl.ds(base, PEER_TOPK), :] = e / jnp.sum(e, axis=0, keepdims=True)

    @pl.when(h == PEER_HEADS - 1)
    def _():
        io_ref[...] = i_scr[...].T
        jo_ref[...] = j_scr[...].T
        go_ref[...] = g_scr[...].T


def _route_call(q, lw):
    t = q.shape[0]
    tb = min(t, 256)
    flat, pen = _cand_tables()
    tokq = pl.BlockSpec((tb, PEER_DKEY), lambda i, h: (i, h))
    keys = pl.BlockSpec((1, 2, PEER_KEYS, PEER_DKEY // 2), lambda i, h: (h, 0, 0, 0))
    out = pl.BlockSpec((tb, PEER_SEL), lambda i, h: (i, 0))
    return pl.pallas_call(
        functools.partial(_route_body, tb), grid=(t // tb, PEER_HEADS),
        in_specs=[tokq, keys, keys, pl.BlockSpec(flat.shape, lambda i, h: (0, 0)),
                  pl.BlockSpec(pen.shape, lambda i, h: (0, 0))],
        out_specs=[out, out, out],
        out_shape=[jax.ShapeDtypeStruct((t, PEER_SEL), I32), jax.ShapeDtypeStruct((t, PEER_SEL), I32),
                   jax.ShapeDtypeStruct((t, PEER_SEL), F32)],
        scratch_shapes=[pltpu.VMEM((2, PEER_TOPK, tb), F32), pltpu.VMEM((2, PEER_TOPK, tb), I32),
                        pltpu.VMEM((PEER_SEL, tb), I32), pltpu.VMEM((PEER_SEL, tb), I32),
                        pltpu.VMEM((PEER_SEL, tb), F32)],
        compiler_params=_cparams(2), name="peer_route",
    )(q, lw["keys_hi"], lw["keys_lo"], flat, pen)


_CHUNK_ROWS = 16
_CHUNK = _CHUNK_ROWS * PEER_KEYS


def _up_body(h_ref, ut_ref, i_ref, j_ref, g_ref, o_ref, val_ref):
    c = pl.program_id(1)

    @pl.when(c == 0)
    def _():
        val_ref[...] = jnp.zeros(val_ref.shape, F32)

    a = _dot(h_ref[...], ut_ref[...])
    iv, jv = i_ref[...], j_ref[...]
    acc = val_ref[...]
    for r in range(_CHUNK_ROWS):
        got = jnp.take_along_axis(a[:, r * PEER_KEYS:(r + 1) * PEER_KEYS], jv, axis=1)
        acc = acc + jnp.where(iv == c * _CHUNK_ROWS + r, got, 0.0)
    val_ref[...] = acc

    @pl.when(c == pl.num_programs(1) - 1)
    def _():
        o_ref[...] = g_ref[...] * (0.5 * acc * (1.0 + lax.erf(acc * (2.0 ** -0.5))))


def _up_call(h2, i_idx, j_idx, gate, lw):
    t = h2.shape[0]
    tb = min(t, 512)
    sel = pl.BlockSpec((tb, PEER_SEL), lambda i, c: (i, 0))
    return pl.pallas_call(
        _up_body, grid=(t // tb, N_EXPERTS // _CHUNK),
        in_specs=[pl.BlockSpec((tb, D_MODEL), lambda i, c: (i, 0)),
                  pl.BlockSpec((D_MODEL, _CHUNK), lambda i, c: (0, c)), sel, sel, sel],
        out_specs=sel,
        out_shape=jax.ShapeDtypeStruct((t, PEER_SEL), F32),
        scratch_shapes=[pltpu.VMEM((tb, PEER_SEL), F32)],
        compiler_params=_cparams(2), name="peer_up",
    )(h2, lw["u_t"], i_idx, j_idx, gate)


def _down_body(final, tb, x_ref, w_ref, i_ref, j_ref, v_ref, gf_ref, o_ref, y_scr, acc_ref):
    c = pl.program_id(1)

    @pl.when(c == 0)
    def _():
        acc_ref[...] = jnp.zeros(acc_ref.shape, F32)
        i3 = lax.broadcasted_iota(I32, (PEER_KEYS, SUBLANES, PEER_SEL), 0)
        t3 = lax.broadcasted_iota(I32, (PEER_KEYS, SUBLANES, PEER_SEL), 1)
        jsub = lax.broadcasted_iota(I32, (PEER_KEYS, PEER_SEL), 0)

        def group(g, carry):
            r0 = pl.multiple_of(g * SUBLANES, SUBLANES)
            wg = w_ref[pl.ds(r0, SUBLANES), :]
            ig = i_ref[pl.ds(r0, SUBLANES), :]
            jg = j_ref[pl.ds(r0, SUBLANES), :]
            xi = jnp.where(ig[None] == i3, wg[None], 0.0)
            lhs = jnp.concatenate([jnp.where(t3 == tp, xi, 0.0) for tp in range(SUBLANES)], axis=2)
            lhs = lhs.reshape(PEER_KEYS * SUBLANES, SUBLANES * PEER_SEL).astype(BF16)
            rhs = jnp.concatenate(
                [jnp.where(jnp.broadcast_to(jg[tp:tp + 1, :], (PEER_KEYS, PEER_SEL)) == jsub, 1.0, 0.0)
                 for tp in range(SUBLANES)], axis=1).astype(BF16)
            res = _dot_nt(lhs, rhs)
            y_scr[:, pl.ds(r0, SUBLANES), :] = res.reshape(PEER_KEYS, SUBLANES, PEER_KEYS)
            return carry

        lax.fori_loop(0, tb // SUBLANES, group, 0)

    base = c * _CHUNK_ROWS
    lhs = jnp.concatenate([y_scr[base + r] for r in range(_CHUNK_ROWS)], axis=1).astype(BF16)
    acc_ref[...] += _dot(lhs, v_ref[...])

    @pl.when(c == pl.num_programs(1) - 1)
    def _():
        xn = x_ref[...] + acc_ref[...]
        if final:
            xn = _rms(xn, gf_ref[...])
        o_ref[...] = xn


def _down_call(x1, wgt, i_idx, j_idx, lw, gfin, final):
    t = x1.shape[0]
    tb = min(t, 256)
    sel = pl.BlockSpec((tb, PEER_SEL), lambda i, c: (i, 0))
    tok = pl.BlockSpec((tb, D_MODEL), lambda i, c: (i, 0))
    return pl.pallas_call(
        functools.partial(_down_body, final, tb), grid=(t // tb, N_EXPERTS // _CHUNK),
        in_specs=[tok, sel, sel, sel, pl.BlockSpec((_CHUNK, D_MODEL), lambda i, c: (c, 0)),
                  pl.BlockSpec((1, D_MODEL), lambda i, c: (0, 0))],
        out_specs=tok,
        out_shape=jax.ShapeDtypeStruct((t, D_MODEL), F32),
        scratch_shapes=[pltpu.VMEM((PEER_KEYS, tb, PEER_KEYS), F32), pltpu.VMEM((tb, D_MODEL), F32)],
        compiler_params=_cparams(2), name="peer_down",
    )(x1, wgt, i_idx, j_idx, lw["v"], gfin)


def _split_bf16(w):
    hi = w.astype(BF16)
    return hi, (w - hi.astype(F32)).astype(BF16)


def _rot_half(w):
    half = MLA_ROPE // 2
    return jnp.concatenate([-w[..., half:], w[..., :half]], axis=-1)


def _prep_layer(l, p):
    w_in = p["w_in"][l]
    kpe_w = w_in[:, IN_W - MLA_ROPE:]
    w_in_ext = jnp.concatenate(
        [w_in[:, :IN_W - MLA_ROPE], jnp.zeros((D_MODEL, MLA_NOPE), F32), kpe_w, _rot_half(kpe_w)], axis=1)
    w_uq = p["mla_w_uq"][l]
    pad = jnp.zeros((MLA_Q, MLA_HEADS, MLA_HEAD_PAD - MLA_NOPE - MLA_ROPE), F32)
    wq = jnp.concatenate([w_uq, pad], axis=-1).reshape(MLA_Q, -1)
    wqr = jnp.concatenate([jnp.zeros((MLA_Q, MLA_HEADS, MLA_NOPE), F32), _rot_half(w_uq[..., MLA_NOPE:]), pad],
                          axis=-1).reshape(MLA_Q, -1)
    w_uk = p["mla_w_uk"][l]
    wk = jnp.concatenate([w_uk, jnp.zeros((MLA_KV, MLA_HEADS, MLA_HEAD_PAD - MLA_NOPE), F32)],
                         axis=-1).reshape(MLA_KV, -1)
    rows = jnp.arange(MLA_HEADS * MLA_HEAD_PAD)
    sel = ((rows[:, None] % MLA_HEAD_PAD) == (MLA_NOPE + jnp.arange(MLA_ROPE))[None, :]).astype(BF16)
    lp = p["diff_lambda"][l].astype(F32)
    lam_init = 0.8 - 0.6 * math.exp(-0.3 * l)
    lam = jnp.exp(jnp.sum(lp[0] * lp[1])) - jnp.exp(jnp.sum(lp[2] * lp[3])) + lam_init
    wq_hi, wq_lo = _split_bf16(p["peer_w_q"][l])
    keys_hi, keys_lo = _split_bf16(p["peer_sub_keys"][l])
    return {
        "g1": p["ln1_g"][l].reshape(1, -1), "g2": p["ln2_g"][l].reshape(1, -1),
        "w_in": w_in_ext.astype(BF16), "conv_w": p["conv_w"][l],
        "qg": p["mla_q_norm_g"][l].reshape(1, -1), "kvg": p["mla_kv_norm_g"][l].reshape(1, -1),
        "wq": wq.astype(BF16), "wqr": wqr.astype(BF16), "wk": wk.astype(BF16),
        "wv": p["mla_w_uv"][l].reshape(MLA_KV, MLA_W).astype(BF16), "sel": sel,
        "lam": lam.reshape(1), "lam11": lam.reshape(1, 1),
        "diff_gain": (jnp.tile(p["diff_norm_g"][l], DIFF_HEADS) * (1.0 - lam_init)).reshape(1, -1),
        "w_out": p["w_out"][l].astype(BF16), "wq_hi": wq_hi, "wq_lo": wq_lo,
        "keys_hi": keys_hi, "keys_lo": keys_lo,
        "u_t": p["peer_u"][l].astype(BF16).T, "v": p["peer_v"][l].astype(BF16),
    }


def _rope_tabs(pos):
    half = MLA_ROPE // 2
    inv = ROPE_THETA ** (-jnp.arange(half, dtype=F32) * (2.0 / MLA_ROPE))
    ang = pos.astype(F32)[:, None] * inv[None, :]
    cos2 = jnp.concatenate([jnp.cos(ang)] * 2, axis=-1)
    sin2 = jnp.concatenate([jnp.sin(ang)] * 2, axis=-1)
    n = pos.shape[0]
    z64 = jnp.zeros((n, MLA_NOPE), F32)
    z32 = jnp.zeros((n, LANES - MLA_NOPE - MLA_ROPE), F32)
    cq = jnp.concatenate([jnp.ones((n, MLA_NOPE), F32), cos2, z32], axis=1)
    ck = jnp.concatenate([z64, cos2, z32], axis=1)
    sn = jnp.concatenate([z64, sin2, z32], axis=1)
    return cq, ck, sn


def _peer_and_residual(x1, h2, q, lw, gfin, final):
    i_idx, j_idx, gate = _route_call(q, lw)
    wgt = _up_call(h2, i_idx, j_idx, gate, lw)
    return _down_call(x1, wgt, i_idx, j_idx, lw, gfin, final)


def _layer_prompt(x, lw, tabs, gfin, final, tq=256):
    b, s, _ = x.shape
    tabs3 = [t[None] for t in tabs]
    (y, dqb, dk, dkb, dv, dvb, ckv, kpe, qm, km, vm, newconv) = _proj_call(x, lw, tabs3)
    tq = min(tq, s)
    d_out = _diff_attn_call(dqb, dkb, dvb, lw["lam"], lw["diff_gain"], tq)
    m_out = _mla_attn_call(qm, km, vm, tq)
    t = b * s
    x1, h2, q = _post_call(x.reshape(t, -1), y.reshape(t, -1), d_out.reshape(t, -1), m_out.reshape(t, -1), lw)
    xn = _peer_and_residual(x1, h2, q, lw, gfin, final)
    return xn.reshape(b, s, -1), (newconv, dk, dv, ckv, kpe)


def _layer_sample(x, lw, tabs, gfin, final, layer, state, caches, page_table, pps=8):
    ns = x.shape[0]
    x3 = x.reshape(1, ns, -1)
    tabs3 = [t[None] for t in tabs]
    st = (state[:, 0, :][None], state[:, 1, :][None])
    (y, dqb, dk, dkb, dv, dvb, ckv, kpe, qm, km, vm, z) = _proj_call(x3, lw, tabs3, st)
    del dkb, dvb, km, vm

    def per_tok(a):
        return a.reshape(ns, 1, -1)

    d_out, m_out = _decode_call(layer, page_table, caches, per_tok(dqb), per_tok(qm), per_tok(dk), per_tok(dv),
                                per_tok(ckv), per_tok(kpe), lw, min(pps, page_table.shape[1]))
    x1, h2, q = _post_call(x.reshape(ns, -1), y.reshape(ns, -1), d_out.reshape(ns, -1), m_out.reshape(ns, -1), lw)
    xn = _peer_and_residual(x1, h2, q, lw, gfin, final)
    newconv = jnp.stack([state[:, 1, :], z.reshape(ns, -1)], axis=1)
    return xn.reshape(ns, 1, -1), (newconv, per_tok(dk), per_tok(dv), per_tok(ckv), per_tok(kpe))


def kernel(x_prompt, x_sample, state_conv, cache_diff_k, cache_diff_v, cache_mla_ckv, cache_mla_kpe,
           page_table, ln1_g, ln2_g, w_in, conv_w, diff_lambda, diff_norm_g, mla_q_norm_g, mla_kv_norm_g,
           mla_w_uq, mla_w_uk, mla_w_uv, w_out, peer_w_q, peer_sub_keys, peer_u, peer_v, final_norm_g):
    params = dict(ln1_g=ln1_g, ln2_g=ln2_g, w_in=w_in, conv_w=conv_w, diff_lambda=diff_lambda,
                  diff_norm_g=diff_norm_g, mla_q_norm_g=mla_q_norm_g, mla_kv_norm_g=mla_kv_norm_g,
                  mla_w_uq=mla_w_uq, mla_w_uk=mla_w_uk, mla_w_uv=mla_w_uv, w_out=w_out,
                  peer_w_q=peer_w_q, peer_sub_keys=peer_sub_keys, peer_u=peer_u, peer_v=peer_v)
    depth = ln1_g.shape[0]
    s = x_prompt.shape[1]
    ns = x_sample.shape[0]
    past_len = page_table.shape[1] * cache_diff_k.shape[2]
    tabs_p = _rope_tabs(jnp.arange(s))
    tabs_s = _rope_tabs(jnp.full((ns,), past_len, I32))
    gfin = final_norm_g.reshape(1, -1)
    caches = (cache_diff_k, cache_diff_v, cache_mla_ckv, cache_mla_kpe)

    xp, xs = x_prompt, x_sample.reshape(ns, -1)
    rows_p, rows_s = [], []
    for l in range(depth):
        lw = _prep_layer(l, params)
        final = l == depth - 1
        xp, new_p = _layer_prompt(xp, lw, tabs_p, gfin, final)
        xs3, new_s = _layer_sample(xs, lw, tabs_s, gfin, final, l, state_conv[l], caches, page_table)
        xs = xs3.reshape(ns, -1)
        rows_p.append(new_p)
        rows_s.append(new_s)
    outs_p = [jnp.stack(r) for r in zip(*rows_p)]
    outs_s = [jnp.stack(r) for r in zip(*rows_s)]
    return (xp, xs.reshape(ns, 1, -1), *outs_p, *outs_s)
```

```python
import functools
import math

import jax
import jax.numpy as jnp
from jax import lax
from jax.experimental import pallas as pl
from jax.experimental.pallas import tpu as pltpu

F32 = jnp.float32
BF16 = jnp.bfloat16
I32 = jnp.int32

D_MODEL = 1024
CONV_W = 256
CONV_K = 3
DIFF_HEADS = 4
DIFF_D = 32
DIFF_V = 64
DIFF_QK = 256
DIFF_W = 256
MLA_HEADS = 8
MLA_NOPE = 64
MLA_ROPE = 32
MLA_V = 64
MLA_KV = 256
MLA_Q = 384
MLA_W = 512
MLA_HEAD_PAD = 128
PEER_HEADS = 8
PEER_KEYS = 128
PEER_DKEY = 256
PEER_TOPK = 16
N_EXPERTS = PEER_KEYS * PEER_KEYS
PEER_SEL = PEER_HEADS * PEER_TOPK
ROPE_THETA = 10000.0
NORM_EPS = 1e-6
NEG_INF = -1e30
PAGE = 128
IN_W = 2208
IN_W_EXT = 2304
LANES = 128
SUBLANES = 8
VMEM_LIMIT = 56 * 1024 * 1024

DIFF_SCALE = DIFF_D ** -0.5
MLA_SCALE = (MLA_NOPE + MLA_ROPE) ** -0.5

_NT = (((1,), (1,)), ((), ()))


def _cparams(n_axes):
    return pltpu.CompilerParams(dimension_semantics=("arbitrary",) * n_axes,
                                vmem_limit_bytes=VMEM_LIMIT)


def _rms(x, g):
    ms = jnp.mean(x * x, axis=-1, keepdims=True)
    return x * lax.rsqrt(ms + NORM_EPS) * g


def _dot(a, b):
    return jnp.dot(a, b, preferred_element_type=F32)


def _dot_nt(a, b):
    return lax.dot_general(a, b, _NT, preferred_element_type=F32)


def _full_spec(shape):
    nd = len(shape)
    return pl.BlockSpec(shape, lambda *_: (0,) * nd)


def _proj_body(decode, tb, *refs):
    (x_ref, g1_ref, win_ref, cw_ref, qg_ref, kvg_ref, wq_ref, wqr_ref, wk_ref, wv_ref,
     cq_ref, ck_ref, sn_ref) = refs[:13]
    rest = refs[13:]
    if decode:
        s0_ref, s1_ref = rest[:2]
        rest = rest[2:]
    (y_ref, dqb_ref, dk_ref, dkb_ref, dv_ref, dvb_ref, ckv_ref, kpe_ref,
     qm_ref, km_ref, vm_ref, zo_ref) = rest[:12]

    x = x_ref[0]
    h = _rms(x, g1_ref[...]).astype(BF16)

    def proj(a, b):
        return _dot(h, win_ref[:, a:b])

    bg = proj(0, 256)
    z = proj(256, 512) * proj(512, 768)
    w0, w1, w2 = cw_ref[0:1, :], cw_ref[1:2, :], cw_ref[2:3, :]
    if decode:
        y = bg * (w0 * s0_ref[0] + w1 * s1_ref[0] + w2 * z)
        zo_ref[0] = z
    else:
        zs_ref = rest[12]
        j = pl.program_id(1)

        @pl.when(j == 0)
        def _():
            zs_ref[0:SUBLANES, :] = jnp.zeros((SUBLANES, CONV_W), F32)

        zs_ref[SUBLANES:SUBLANES + tb, :] = z
        zm1 = zs_ref[SUBLANES - 1:SUBLANES - 1 + tb, :]
        zm2 = zs_ref[SUBLANES - 2:SUBLANES - 2 + tb, :]
        y = bg * (w0 * zm2 + w1 * zm1 + w2 * z)
        zs_ref[0:SUBLANES, :] = z[tb - SUBLANES:tb, :]
        zo_ref[0] = z[tb - 2:tb, :]
    y_ref[0] = y.astype(BF16)

    dqb_ref[0] = (proj(768, 1024) * DIFF_SCALE).astype(BF16)
    dk = proj(1024, 1280)
    dk_ref[0] = dk
    dkb_ref[0] = dk.astype(BF16)
    dv = proj(1280, 1536)
    dv_ref[0] = dv
    dvb_ref[0] = dv.astype(BF16)

    cqn = _rms(proj(1536, 1920), qg_ref[...]).astype(BF16)
    cq8 = jnp.concatenate([cq_ref[0]] * MLA_HEADS, axis=1)
    sn8 = jnp.concatenate([sn_ref[0]] * MLA_HEADS, axis=1)
    qm = (_dot(cqn, wq_ref[...]) * cq8 + _dot(cqn, wqr_ref[...]) * sn8) * MLA_SCALE
    qm_ref[0] = qm.astype(BF16)

    ckvn = _rms(proj(1920, 2176), kvg_ref[...])
    ckv_ref[0] = ckvn
    ckb = ckvn.astype(BF16)
    tl = proj(2176, 2304)
    kr = tl * ck_ref[0] + pltpu.roll(tl, LANES - MLA_ROPE, 1) * sn_ref[0]
    kpe_ref[0] = kr[:, MLA_NOPE:MLA_NOPE + MLA_ROPE]
    km = _dot(ckb, wk_ref[...]) + jnp.concatenate([kr] * MLA_HEADS, axis=1)
    km_ref[0] = km.astype(BF16)
    vm_ref[0] = _dot(ckb, wv_ref[...]).astype(BF16)


def _proj_call(x3, lw, tabs, state=None):
    b, s, _ = x3.shape
    decode = state is not None
    tb = min(s, 512)
    assert s % tb == 0
    grid = (b, s // tb)

    def tok(w):
        return pl.BlockSpec((1, tb, w), lambda i, j: (i, j, 0))

    weights = [lw["g1"], lw["w_in"], lw["conv_w"], lw["qg"], lw["kvg"], lw["wq"], lw["wqr"], lw["wk"], lw["wv"]]
    tab = pl.BlockSpec((1, tb, LANES), lambda i, j: (0, j, 0))
    in_specs = [tok(D_MODEL)] + [_full_spec(w.shape) for w in weights] + [tab] * 3
    args = [x3] + weights + list(tabs)
    if decode:
        in_specs += [tok(CONV_W)] * 2
        args += list(state)

    def sds(w, dt):
        return jax.ShapeDtypeStruct((b, s, w), dt)

    out_shape = [sds(CONV_W, BF16), sds(DIFF_QK, BF16), sds(DIFF_QK, F32), sds(DIFF_QK, BF16),
                 sds(DIFF_W, F32), sds(DIFF_W, BF16), sds(MLA_KV, F32), sds(MLA_ROPE, F32),
                 sds(MLA_HEADS * MLA_HEAD_PAD, BF16), sds(MLA_HEADS * MLA_HEAD_PAD, BF16), sds(MLA_W, BF16)]
    out_specs = [tok(CONV_W), tok(DIFF_QK), tok(DIFF_QK), tok(DIFF_QK), tok(DIFF_W), tok(DIFF_W),
                 tok(MLA_KV), tok(MLA_ROPE), tok(MLA_HEADS * MLA_HEAD_PAD), tok(MLA_HEADS * MLA_HEAD_PAD),
                 tok(MLA_W)]
    scratch = []
    if decode:
        out_shape.append(sds(CONV_W, F32))
        out_specs.append(tok(CONV_W))
    else:
        out_shape.append(jax.ShapeDtypeStruct((b, CONV_K - 1, CONV_W), F32))
        out_specs.append(pl.BlockSpec((1, CONV_K - 1, CONV_W), lambda i, j: (i, 0, 0)))
        scratch.append(pltpu.VMEM((tb + SUBLANES, CONV_W), F32))
    return pl.pallas_call(
        functools.partial(_proj_body, decode, tb),
        grid=grid, in_specs=in_specs, out_specs=out_specs, out_shape=out_shape,
        scratch_shapes=scratch, compiler_params=_cparams(2), name="proj",
    )(*args)


def _flash_step(q, k, v, mask, m_ref, l_ref, acc_ref):
    s = _dot_nt(q, k)
    if mask is not None:
        s = jnp.where(mask, s, NEG_INF)
    tiles = [s[:, c * LANES:(c + 1) * LANES] for c in range(s.shape[1] // LANES)]
    m_prev = m_ref[...]
    row_max = jnp.max(functools.reduce(jnp.maximum, tiles), axis=-1, keepdims=True)
    m_new = jnp.maximum(m_prev, row_max)
    alpha = jnp.exp(m_prev - m_new)
    p_tiles = [jnp.exp(t - m_new) for t in tiles]
    l_ref[...] = alpha * l_ref[...] + functools.reduce(jnp.add, p_tiles)
    p = jnp.concatenate(p_tiles, axis=1).astype(BF16)
    acc_ref[...] = alpha * acc_ref[...] + _dot(p, v)
    m_ref[...] = m_new


def _init_state(m_ref, l_ref, acc_ref):
    m_ref[...] = jnp.full(m_ref.shape, NEG_INF, F32)
    l_ref[...] = jnp.zeros(l_ref.shape, F32)
    acc_ref[...] = jnp.zeros(acc_ref.shape, F32)


def _causal_mask(rows, tq, tk, i, j):
    row = lax.broadcasted_iota(I32, (rows, tk), 0) & (tq - 1)
    col = lax.broadcasted_iota(I32, (rows, tk), 1)
    return col + j * tk <= row + i * tq


def _causal_sweep(tq, tk, i, step):
    assert tk % tq == 0 and tq & (tq - 1) == 0
    j_diag = i // (tk // tq)

    def body(j, c):
        step(j, False)
        return c

    lax.fori_loop(0, j_diag, body, 0)
    step(j_diag, True)


def _diff_attn_body(tq, tk, lam_ref, q_ref, k_ref, v_ref, g_ref, o_ref, m_ref, l_ref, acc_ref):
    i = pl.program_id(2)
    q = q_ref[0].astype(F32)
    lane = lax.broadcasted_iota(I32, (1, LANES), 1)
    seg = lane >> 5
    qs = jnp.concatenate([jnp.where(seg == g, q, 0.0) for g in range(4)], axis=0).astype(BF16)
    _init_state(m_ref, l_ref, acc_ref)

    def step(j, masked):
        off = pl.multiple_of(j * tk, tk)
        k = k_ref[0, pl.ds(off, tk), :]
        v = v_ref[0, pl.ds(off, tk), :]
        mask = _causal_mask(4 * tq, tq, tk, i, j) if masked else None
        _flash_step(qs, k, v, mask, m_ref, l_ref, acc_ref)

    _causal_sweep(tq, tk, i, step)

    lam = lam_ref[0]
    o_all = acc_ref[...] / jnp.sum(l_ref[...], axis=-1, keepdims=True)
    o0 = o_all[0:tq] - lam * o_all[tq:2 * tq]
    o1 = o_all[2 * tq:3 * tq] - lam * o_all[3 * tq:4 * tq]
    first = lane < DIFF_V
    o = jnp.where(first, o0, o1)
    sq = o * o
    ss0 = jnp.sum(jnp.where(first, sq, 0.0), axis=-1, keepdims=True)
    ss1 = jnp.sum(jnp.where(first, 0.0, sq), axis=-1, keepdims=True)
    ms = jnp.where(first, ss0, ss1) * (1.0 / DIFF_V)
    o_ref[0] = (o * lax.rsqrt(ms + NORM_EPS) * g_ref[...]).astype(BF16)


def _diff_attn_call(dq, dk, dv, lam, gain, tq, tk):
    b, s, _ = dq.shape
    n_pairs = DIFF_QK // LANES
    grid = (b, n_pairs, s // tq)
    return pl.pallas_call(
        functools.partial(_diff_attn_body, tq, tk),
        grid=grid,
        in_specs=[pl.BlockSpec(memory_space=pltpu.SMEM),
                  pl.BlockSpec((1, tq, LANES), lambda bi, hp, i: (bi, i, hp)),
                  pl.BlockSpec((1, s, LANES), lambda bi, hp, i: (bi, 0, hp)),
                  pl.BlockSpec((1, s, LANES), lambda bi, hp, i: (bi, 0, hp)),
                  pl.BlockSpec((1, LANES), lambda bi, hp, i: (0, hp))],
        out_specs=pl.BlockSpec((1, tq, LANES), lambda bi, hp, i: (bi, i, hp)),
        out_shape=jax.ShapeDtypeStruct((b, s, DIFF_W), BF16),
        scratch_shapes=[pltpu.VMEM((4 * tq, LANES), F32)] * 3,
        compiler_params=_cparams(3), name="diff_attn",
    )(lam, dq, dk, dv, gain)


def _mla_attn_body(tq, tk, q_ref, k_ref, v_ref, o_ref, m_ref, l_ref, acc_ref):
    i = pl.program_id(2)
    q = q_ref[0]
    _init_state(m_ref, l_ref, acc_ref)

    def step(j, masked):
        off = pl.multiple_of(j * tk, tk)
        k = k_ref[0, pl.ds(off, tk), :]
        v = v_ref[0, pl.ds(off, tk), :]
        mask = _causal_mask(tq, tq, tk, i, j) if masked else None
        for hh in range(2):
            sl = slice(hh * MLA_HEAD_PAD, (hh + 1) * MLA_HEAD_PAD)
            _flash_step(q[:, sl], k[:, sl], v, mask, m_ref.at[hh], l_ref.at[hh], acc_ref.at[hh])

    _causal_sweep(tq, tk, i, step)
    lane = lax.broadcasted_iota(I32, (1, LANES), 1)
    o = acc_ref[...] / jnp.sum(l_ref[...], axis=-1, keepdims=True)
    o_ref[0] = jnp.where(lane < MLA_V, o[0], o[1]).astype(BF16)


def _mla_attn_call(qm, km, vm, tq, tk):
    b, s, _ = qm.shape
    n_pairs = MLA_HEADS // 2
    grid = (b, n_pairs, s // tq)
    qw = 2 * MLA_HEAD_PAD
    return pl.pallas_call(
        functools.partial(_mla_attn_body, tq, tk),
        grid=grid,
        in_specs=[pl.BlockSpec((1, tq, qw), lambda bi, hp, i: (bi, i, hp)),
                  pl.BlockSpec((1, s, qw), lambda bi, hp, i: (bi, 0, hp)),
                  pl.BlockSpec((1, s, LANES), lambda bi, hp, i: (bi, 0, hp))],
        out_specs=pl.BlockSpec((1, tq, LANES), lambda bi, hp, i: (bi, i, hp)),
        out_shape=jax.ShapeDtypeStruct((b, s, MLA_W), BF16),
        scratch_shapes=[pltpu.VMEM((2, tq, LANES), F32)] * 3,
        compiler_params=_cparams(3), name="mla_attn",
    )(qm, km, vm)


def _decode_body(pps, *refs):
    pt_ref = refs[0]
    del pt_ref
    (dq_ref, qm_ref, dks_ref, dvs_ref, ckvs_ref, kpes_ref, wk_ref, sel_ref, wuv_ref,
     gain_ref, lam_ref) = refs[1:12]
    pages = refs[12:12 + 4 * pps]
    kd_refs, vd_refs = pages[:pps], pages[pps:2 * pps]
    ckv_refs, kpe_refs = pages[2 * pps:3 * pps], pages[3 * pps:]
    do_ref, mo_ref = refs[12 + 4 * pps:14 + 4 * pps]
    (qbd_ref, qlat_ref, qpe_ref, md_ref, ld_ref, accd_ref, mm_ref, lm_ref, accm_ref) = refs[14 + 4 * pps:]

    p = pl.program_id(1)
    n_steps = pl.num_programs(1)
    n_rows = SUBLANES

    @pl.when(p == 0)
    def _():
        sub = lax.broadcasted_iota(I32, (n_rows, DIFF_QK), 0)
        lane = lax.broadcasted_iota(I32, (n_rows, DIFF_QK), 1)
        row = jnp.broadcast_to(dq_ref[0].astype(F32), (n_rows, DIFF_QK))
        qbd = jnp.where((lane >> 5) == sub, row, 0.0)
        qbd_ref[...] = qbd.astype(BF16)
        md_ref[...] = jnp.sum(qbd * dks_ref[0], axis=1, keepdims=True)
        ld_ref[...] = jnp.ones((n_rows, 1), F32)
        accd_ref[...] = jnp.broadcast_to(dvs_ref[0], (n_rows, DIFF_W))

        wq = MLA_HEADS * MLA_HEAD_PAD
        sub = lax.broadcasted_iota(I32, (n_rows, wq), 0)
        lane = lax.broadcasted_iota(I32, (n_rows, wq), 1)
        row = jnp.broadcast_to(qm_ref[0].astype(F32), (n_rows, wq))
        qf = jnp.where((lane >> 7) == sub, row, 0.0).astype(BF16)
        qlat = _dot_nt(qf, wk_ref[...]).astype(BF16)
        qpe = _dot(qf, sel_ref[...]).astype(BF16)
        qlat_ref[...] = qlat
        qpe_ref[...] = qpe
        mm_ref[...] = (jnp.sum(qlat.astype(F32) * ckvs_ref[0], axis=1, keepdims=True)
                       + jnp.sum(qpe.astype(F32) * kpes_ref[0], axis=1, keepdims=True))
        lm_ref[...] = jnp.ones((n_rows, 1), F32)
        accm_ref[...] = jnp.broadcast_to(ckvs_ref[0], (n_rows, MLA_KV))

    def cat(rs):
        return jnp.concatenate([r[0, 0] for r in rs], axis=0).astype(BF16)

    def update(s, v, m_ref, l_ref, acc_ref):
        m_prev = m_ref[...]
        m_new = jnp.maximum(m_prev, jnp.max(s, axis=-1, keepdims=True))
        alpha = jnp.exp(m_prev - m_new)
        pr = jnp.exp(s - m_new)
        l_ref[...] = alpha * l_ref[...] + jnp.sum(pr, axis=-1, keepdims=True)
        acc_ref[...] = alpha * acc_ref[...] + _dot(pr.astype(BF16), v)
        m_ref[...] = m_new

    update(_dot_nt(qbd_ref[...], cat(kd_refs)), cat(vd_refs), md_ref, ld_ref, accd_ref)
    c = cat(ckv_refs)
    update(_dot_nt(qlat_ref[...], c) + _dot_nt(qpe_ref[...], cat(kpe_refs)), c, mm_ref, lm_ref, accm_ref)

    @pl.when(p == n_steps - 1)
    def _():
        lam = lam_ref[...]
        sub = lax.broadcasted_iota(I32, (n_rows, DIFF_W), 0)
        lane = lax.broadcasted_iota(I32, (n_rows, DIFF_W), 1)
        coef = jnp.where((sub & 1) == 0, 1.0, -lam)
        o2 = accd_ref[...] / ld_ref[...] * coef
        orow = jnp.sum(jnp.where((lane >> 6) == (sub >> 1), o2, 0.0), axis=0, keepdims=True)
        lane1 = lane[0:1, :] >> 6
        sq = orow * orow
        ms = jnp.zeros_like(orow)
        for g in range(DIFF_HEADS):
            ssg = jnp.sum(jnp.where(lane1 == g, sq, 0.0), axis=-1, keepdims=True)
            ms = jnp.where(lane1 == g, ssg, ms)
        ms = ms * (1.0 / DIFF_V)
        do_ref[0] = (orow * lax.rsqrt(ms + NORM_EPS) * gain_ref[...]).astype(BF16)

        olat = (accm_ref[...] / lm_ref[...]).astype(BF16)
        r = _dot(olat, wuv_ref[...])
        sub = lax.broadcasted_iota(I32, (n_rows, MLA_W), 0)
        lane = lax.broadcasted_iota(I32, (n_rows, MLA_W), 1)
        mo_ref[0] = jnp.sum(jnp.where((lane >> 6) == sub, r, 0.0), axis=0, keepdims=True).astype(BF16)


def _decode_call(layer, page_table, caches, dq, qm, dks, dvs, ckvs, kpes, lw, pps):
    ns, n_pages = page_table.shape
    assert n_pages % pps == 0
    n_steps = n_pages // pps
    cache_k, cache_v, cache_ckv, cache_kpe = caches

    def tok(w):
        return pl.BlockSpec((1, 1, w), lambda bi, p, pt: (bi, 0, 0))

    def full(shape):
        nd = len(shape)
        return pl.BlockSpec(shape, lambda bi, p, pt: (0,) * nd)

    def page(w, r):
        return pl.BlockSpec((1, 1, PAGE, w),
                            lambda bi, p, pt: (layer, pt[bi * n_pages + p * pps + r], 0, 0))

    weights = [lw["wk"], lw["sel"], lw["wv"], lw["diff_gain"], lw["lam11"]]
    in_specs = ([tok(DIFF_QK), tok(MLA_HEADS * MLA_HEAD_PAD), tok(DIFF_QK), tok(DIFF_W), tok(MLA_KV), tok(MLA_ROPE)]
                + [full(w.shape) for w in weights]
                + [page(DIFF_QK, r) for r in range(pps)] + [page(DIFF_W, r) for r in range(pps)]
                + [page(MLA_KV, r) for r in range(pps)] + [page(MLA_ROPE, r) for r in range(pps)])
    args = ([dq, qm, dks, dvs, ckvs, kpes] + weights
            + [cache_k] * pps + [cache_v] * pps + [cache_ckv] * pps + [cache_kpe] * pps)
    grid_spec = pltpu.PrefetchScalarGridSpec(
        num_scalar_prefetch=1, grid=(ns, n_steps), in_specs=in_specs,
        out_specs=[tok(DIFF_W), tok(MLA_W)],
        scratch_shapes=[pltpu.VMEM((SUBLANES, DIFF_QK), BF16), pltpu.VMEM((SUBLANES, MLA_KV), BF16),
                        pltpu.VMEM((SUBLANES, MLA_ROPE), BF16),
                        pltpu.VMEM((SUBLANES, 1), F32), pltpu.VMEM((SUBLANES, 1), F32),
                        pltpu.VMEM((SUBLANES, DIFF_W), F32),
                        pltpu.VMEM((SUBLANES, 1), F32), pltpu.VMEM((SUBLANES, 1), F32),
                        pltpu.VMEM((SUBLANES, MLA_KV), F32)])
    return pl.pallas_call(
        functools.partial(_decode_body, pps),
        grid_spec=grid_spec,
        out_shape=[jax.ShapeDtypeStruct((ns, 1, DIFF_W), BF16), jax.ShapeDtypeStruct((ns, 1, MLA_W), BF16)],
        compiler_params=_cparams(2), name="decode_attn",
    )(page_table.reshape(-1), *args)


def _post_body(x_ref, y_ref, d_ref, m_ref, wo_ref, g2_ref, wqh_ref, wql_ref, x1_ref, h2_ref, q_ref):
    x1 = (x_ref[...] + _dot(y_ref[...], wo_ref[0:CONV_W, :])
          + _dot(d_ref[...], wo_ref[CONV_W:CONV_W + DIFF_W, :])
          + _dot(m_ref[...], wo_ref[CONV_W + DIFF_W:, :]))
    x1_ref[...] = x1
    h2 = _rms(x1, g2_ref[...])
    hb = h2.astype(BF16)
    hl = (h2 - hb.astype(F32)).astype(BF16)
    q_ref[...] = _dot(hb, wqh_ref[...]) + _dot(hb, wql_ref[...]) + _dot(hl, wqh_ref[...])
    h2_ref[...] = hb


def _post_call(x, y, d, m, lw):
    t = x.shape[0]
    tb = min(t, 256)
    qw = PEER_HEADS * PEER_DKEY

    def tok(w):
        return pl.BlockSpec((tb, w), lambda i: (i, 0))

    weights = [lw["w_out"], lw["g2"], lw["wq_hi"], lw["wq_lo"]]
    return pl.pallas_call(
        _post_body, grid=(t // tb,),
        in_specs=[tok(D_MODEL), tok(CONV_W), tok(DIFF_W), tok(MLA_W)] + [_full_spec(w.shape) for w in weights],
        out_specs=[tok(D_MODEL), tok(D_MODEL), tok(qw)],
        out_shape=[jax.ShapeDtypeStruct((t, D_MODEL), F32), jax.ShapeDtypeStruct((t, D_MODEL), BF16),
                   jax.ShapeDtypeStruct((t, qw), F32)],
        compiler_params=_cparams(1), name="post_attn",
    )(x, y, d, m, *weights)


_CAND_ROWS = PEER_TOPK + SUBLANES * (PEER_TOPK - 1)


def _cand_tables():
    flat, pen = [], []
    for a in range(PEER_TOPK):
        nb = PEER_TOPK if a == 0 else SUBLANES
        for b_ in range(nb):
            flat.append(a * PEER_TOPK + b_)
            pen.append(0.0 if (a + 1) * (b_ + 1) <= PEER_TOPK else -math.inf)
    return (jnp.asarray(flat, I32).reshape(-1, 1), jnp.asarray(pen, F32).reshape(-1, 1))


def _route_body(tb, q_ref, khi_ref, klo_ref, flat_ref, pen_ref, io_ref, jo_ref, go_ref,
                sv_ref, si_ref, i_scr, j_scr, g_scr):
    h = pl.program_id(1)
    q = q_ref[...]
    qh = q.astype(BF16)
    ql = (q - qh.astype(F32)).astype(BF16)
    n_iota = lax.broadcasted_iota(I32, (PEER_KEYS, tb), 0)

    for p in range(2):
        sl = slice(p * PEER_KEYS, (p + 1) * PEER_KEYS)
        kh, kl = khi_ref[0, p], klo_ref[0, p]
        s0 = _dot_nt(kh, qh[:, sl]) + _dot_nt(kh, ql[:, sl]) + _dot_nt(kl, qh[:, sl])

        def it(k, s, p=p):
            m = jnp.max(s, axis=0, keepdims=True)
            idx = jnp.min(jnp.where(s == m, n_iota, PEER_KEYS), axis=0, keepdims=True)
            sv_ref[p, pl.ds(k, 1), :] = m
            si_ref[p, pl.ds(k, 1), :] = idx
            return jnp.where(n_iota == idx, -jnp.inf, s)

        lax.fori_loop(0, PEER_TOPK, it, s0)

    sv0, sv1 = sv_ref[0], sv_ref[1]
    si0, si1 = si_ref[0], si_ref[1]
    parts = [sv0[0:1, :] + sv1]
    for a in range(1, PEER_TOPK):
        parts.append(sv0[a:a + 1, :] + sv1[0:SUBLANES, :])
    cand0 = jnp.concatenate(parts, axis=0) + pen_ref[...]
    flat = jnp.broadcast_to(flat_ref[...], (_CAND_ROWS, tb))
    k_iota = lax.broadcasted_iota(I32, (PEER_TOPK, tb), 0)
    base = pl.multiple_of(h * PEER_TOPK, PEER_TOPK)

    def it2(k, cand):
        m = jnp.max(cand, axis=0, keepdims=True)
        fl = jnp.min(jnp.where(cand == m, flat, PEER_TOPK * PEER_TOPK), axis=0, keepdims=True)
        a = fl >> 4
        b_ = fl & (PEER_TOPK - 1)
        g_scr[pl.ds(base + k, 1), :] = m
        i_scr[pl.ds(base + k, 1), :] = jnp.sum(jnp.where(k_iota == a, si0, 0), axis=0, keepdims=True)
        j_scr[pl.ds(base + k, 1), :] = jnp.sum(jnp.where(k_iota == b_, si1, 0), axis=0, keepdims=True)
        return jnp.where(flat == fl, -jnp.inf, cand)

    lax.fori_loop(0, PEER_TOPK, it2, cand0)
    ts = g_scr[pl.ds(base, PEER_TOPK), :]
    e = jnp.exp(ts - ts[0:1, :])
    g_scr[pl.ds(base, PEER_TOPK), :] = e / jnp.sum(e, axis=0, keepdims=True)

    @pl.when(h == PEER_HEADS - 1)
    def _():
        io_ref[...] = i_scr[...].T
        jo_ref[...] = j_scr[...].T
        go_ref[...] = g_scr[...].T


def _route_call(q, lw):
    t = q.shape[0]
    tb = min(t, 256)
    flat, pen = _cand_tables()
    tokq = pl.BlockSpec((tb, PEER_DKEY), lambda i, h: (i, h))
    keys = pl.BlockSpec((1, 2, PEER_KEYS, PEER_DKEY // 2), lambda i, h: (h, 0, 0, 0))
    out = pl.BlockSpec((tb, PEER_SEL), lambda i, h: (i, 0))
    return pl.pallas_call(
        functools.partial(_route_body, tb), grid=(t // tb, PEER_HEADS),
        in_specs=[tokq, keys, keys, pl.BlockSpec(flat.shape, lambda i, h: (0, 0)),
                  pl.BlockSpec(pen.shape, lambda i, h: (0, 0))],
        out_specs=[out, out, out],
        out_shape=[jax.ShapeDtypeStruct((t, PEER_SEL), I32), jax.ShapeDtypeStruct((t, PEER_SEL), I32),
                   jax.ShapeDtypeStruct((t, PEER_SEL), F32)],
        scratch_shapes=[pltpu.VMEM((2, PEER_TOPK, tb), F32), pltpu.VMEM((2, PEER_TOPK, tb), I32),
                        pltpu.VMEM((PEER_SEL, tb), I32), pltpu.VMEM((PEER_SEL, tb), I32),
                        pltpu.VMEM((PEER_SEL, tb), F32)],
        compiler_params=_cparams(2), name="peer_route",
    )(q, lw["keys_hi"], lw["keys_lo"], flat, pen)


_CHUNK_ROWS = 16
_CHUNK = _CHUNK_ROWS * PEER_KEYS


def _up_body(h_ref, u_ref, i_ref, j_ref, g_ref, o_ref, val_ref):
    c = pl.program_id(1)

    @pl.when(c == 0)
    def _():
        val_ref[...] = jnp.zeros(val_ref.shape, F32)

    a = _dot_nt(h_ref[...], u_ref[...])
    iv, jv = i_ref[...], j_ref[...]
    acc = val_ref[...]
    for r in range(_CHUNK_ROWS):
        got = jnp.take_along_axis(a[:, r * PEER_KEYS:(r + 1) * PEER_KEYS], jv, axis=1)
        acc = acc + jnp.where(iv == c * _CHUNK_ROWS + r, got, 0.0)
    val_ref[...] = acc

    @pl.when(c == pl.num_programs(1) - 1)
    def _():
        o_ref[...] = g_ref[...] * (0.5 * acc * (1.0 + lax.erf(acc * (2.0 ** -0.5))))


def _up_call(h2, i_idx, j_idx, gate, lw):
    t = h2.shape[0]
    tb = min(t, 512)
    sel = pl.BlockSpec((tb, PEER_SEL), lambda i, c: (i, 0))
    return pl.pallas_call(
        _up_body, grid=(t // tb, N_EXPERTS // _CHUNK),
        in_specs=[pl.BlockSpec((tb, D_MODEL), lambda i, c: (i, 0)),
                  pl.BlockSpec((_CHUNK, D_MODEL), lambda i, c: (c, 0)), sel, sel, sel],
        out_specs=sel,
        out_shape=jax.ShapeDtypeStruct((t, PEER_SEL), F32),
        scratch_shapes=[pltpu.VMEM((tb, PEER_SEL), F32)],
        compiler_params=_cparams(2), name="peer_up",
    )(h2, lw["u"], i_idx, j_idx, gate)


_Y_PITCH = PEER_KEYS + SUBLANES


def _down_body(final, tb, x_ref, w_ref, i_ref, j_ref, v_ref, gf_ref, o_ref, y_scr, acc_ref):
    c = pl.program_id(1)

    @pl.when(c == 0)
    def _():
        acc_ref[...] = jnp.zeros(acc_ref.shape, F32)
        shape = (PEER_KEYS, PEER_SEL)
        sub = lax.broadcasted_iota(I32, shape, 0)

        def token(t, carry):
            wrow = jnp.broadcast_to(w_ref[pl.ds(t, 1), :], shape)
            irow = jnp.broadcast_to(i_ref[pl.ds(t, 1), :], shape)
            jrow = jnp.broadcast_to(j_ref[pl.ds(t, 1), :], shape)
            lhs = jnp.where(irow == sub, wrow, 0.0).astype(BF16)
            rhs = jnp.where(jrow == sub, 1.0, 0.0).astype(BF16)
            y_scr[pl.ds(pl.multiple_of(t * _Y_PITCH, SUBLANES), PEER_KEYS), :] = _dot_nt(lhs, rhs)
            return carry

        lax.fori_loop(0, tb, token, 0, unroll=8)

    base = c * _CHUNK_ROWS
    lhs = jnp.concatenate([y_scr[pl.ds(base + r, tb, stride=_Y_PITCH), :] for r in range(_CHUNK_ROWS)],
                          axis=1).astype(BF16)
    acc_ref[...] += _dot(lhs, v_ref[...])

    @pl.when(c == pl.num_programs(1) - 1)
    def _():
        xn = x_ref[...] + acc_ref[...]
        if final:
            xn = _rms(xn, gf_ref[...])
        o_ref[...] = xn


def _down_call(x1, wgt, i_idx, j_idx, lw, gfin, final):
    t = x1.shape[0]
    tb = min(t, 256)
    sel = pl.BlockSpec((tb, PEER_SEL), lambda i, c: (i, 0))
    tok = pl.BlockSpec((tb, D_MODEL), lambda i, c: (i, 0))
    return pl.pallas_call(
        functools.partial(_down_body, final, tb), grid=(t // tb, N_EXPERTS // _CHUNK),
        in_specs=[tok, sel, sel, sel, pl.BlockSpec((_CHUNK, D_MODEL), lambda i, c: (c, 0)),
                  pl.BlockSpec((1, D_MODEL), lambda i, c: (0, 0))],
        out_specs=tok,
        out_shape=jax.ShapeDtypeStruct((t, D_MODEL), F32),
        scratch_shapes=[pltpu.VMEM((tb * _Y_PITCH, PEER_KEYS), F32), pltpu.VMEM((tb, D_MODEL), F32)],
        compiler_params=_cparams(2), name="peer_down",
    )(x1, wgt, i_idx, j_idx, lw["v"], gfin)


def _split_bf16(w):
    hi = w.astype(BF16)
    return hi, (w - hi.astype(F32)).astype(BF16)


def _rot_half(w):
    half = MLA_ROPE // 2
    return jnp.concatenate([-w[..., half:], w[..., :half]], axis=-1)


def _prep_layer(l, p):
    w_in = p["w_in"][l]
    kpe_w = w_in[:, IN_W - MLA_ROPE:]
    w_in_ext = jnp.concatenate(
        [w_in[:, :IN_W - MLA_ROPE], jnp.zeros((D_MODEL, MLA_NOPE), F32), kpe_w, _rot_half(kpe_w)], axis=1)
    w_uq = p["mla_w_uq"][l]
    pad = jnp.zeros((MLA_Q, MLA_HEADS, MLA_HEAD_PAD - MLA_NOPE - MLA_ROPE), F32)
    wq = jnp.concatenate([w_uq, pad], axis=-1).reshape(MLA_Q, -1)
    wqr = jnp.concatenate([jnp.zeros((MLA_Q, MLA_HEADS, MLA_NOPE), F32), _rot_half(w_uq[..., MLA_NOPE:]), pad],
                          axis=-1).reshape(MLA_Q, -1)
    w_uk = p["mla_w_uk"][l]
    wk = jnp.concatenate([w_uk, jnp.zeros((MLA_KV, MLA_HEADS, MLA_HEAD_PAD - MLA_NOPE), F32)],
                         axis=-1).reshape(MLA_KV, -1)
    rows = jnp.arange(MLA_HEADS * MLA_HEAD_PAD)
    sel = ((rows[:, None] % MLA_HEAD_PAD) == (MLA_NOPE + jnp.arange(MLA_ROPE))[None, :]).astype(BF16)
    lp = p["diff_lambda"][l].astype(F32)
    lam_init = 0.8 - 0.6 * math.exp(-0.3 * l)
    lam = jnp.exp(jnp.sum(lp[0] * lp[1])) - jnp.exp(jnp.sum(lp[2] * lp[3])) + lam_init
    wq_hi, wq_lo = _split_bf16(p["peer_w_q"][l])
    keys_hi, keys_lo = _split_bf16(p["peer_sub_keys"][l])
    return {
        "g1": p["ln1_g"][l].reshape(1, -1), "g2": p["ln2_g"][l].reshape(1, -1),
        "w_in": w_in_ext.astype(BF16), "conv_w": p["conv_w"][l],
        "qg": p["mla_q_norm_g"][l].reshape(1, -1), "kvg": p["mla_kv_norm_g"][l].reshape(1, -1),
        "wq": wq.astype(BF16), "wqr": wqr.astype(BF16), "wk": wk.astype(BF16),
        "wv": p["mla_w_uv"][l].reshape(MLA_KV, MLA_W).astype(BF16), "sel": sel,
        "lam": lam.reshape(1), "lam11": lam.reshape(1, 1),
        "diff_gain": (jnp.tile(p["diff_norm_g"][l], DIFF_HEADS) * (1.0 - lam_init)).reshape(1, -1),
        "w_out": p["w_out"][l].astype(BF16), "wq_hi": wq_hi, "wq_lo": wq_lo,
        "keys_hi": keys_hi, "keys_lo": keys_lo,
        "u": p["peer_u"][l].astype(BF16), "v": p["peer_v"][l].astype(BF16),
    }


def _rope_tabs(pos):
    half = MLA_ROPE // 2
    inv = ROPE_THETA ** (-jnp.arange(half, dtype=F32) * (2.0 / MLA_ROPE))
    ang = pos.astype(F32)[:, None] * inv[None, :]
    cos2 = jnp.concatenate([jnp.cos(ang)] * 2, axis=-1)
    sin2 = jnp.concatenate([jnp.sin(ang)] * 2, axis=-1)
    n = pos.shape[0]
    z64 = jnp.zeros((n, MLA_NOPE), F32)
    z32 = jnp.zeros((n, LANES - MLA_NOPE - MLA_ROPE), F32)
    cq = jnp.concatenate([jnp.ones((n, MLA_NOPE), F32), cos2, z32], axis=1)
    ck = jnp.concatenate([z64, cos2, z32], axis=1)
    sn = jnp.concatenate([z64, sin2, z32], axis=1)
    return cq, ck, sn


def _peer_and_residual(x1, h2, q, lw, gfin, final):
    i_idx, j_idx, gate = _route_call(q, lw)
    wgt = _up_call(h2, i_idx, j_idx, gate, lw)
    return _down_call(x1, wgt, i_idx, j_idx, lw, gfin, final)


def _layer_prompt(x, lw, tabs, gfin, final, tq_diff=256, tq_mla=512, tk=512):
    b, s, _ = x.shape
    tabs3 = [t[None] for t in tabs]
    (y, dqb, dk, dkb, dv, dvb, ckv, kpe, qm, km, vm, newconv) = _proj_call(x, lw, tabs3)
    tk = min(tk, s)
    d_out = _diff_attn_call(dqb, dkb, dvb, lw["lam"], lw["diff_gain"], min(tq_diff, s), tk)
    m_out = _mla_attn_call(qm, km, vm, min(tq_mla, s), tk)
    t = b * s
    x1, h2, q = _post_call(x.reshape(t, -1), y.reshape(t, -1), d_out.reshape(t, -1), m_out.reshape(t, -1), lw)
    xn = _peer_and_residual(x1, h2, q, lw, gfin, final)
    return xn.reshape(b, s, -1), (newconv, dk, dv, ckv, kpe)


def _layer_sample(x, lw, tabs, gfin, final, layer, state, caches, page_table, pps=16):
    ns = x.shape[0]
    x3 = x.reshape(1, ns, -1)
    tabs3 = [t[None] for t in tabs]
    st = (state[:, 0, :][None], state[:, 1, :][None])
    (y, dqb, dk, dkb, dv, dvb, ckv, kpe, qm, km, vm, z) = _proj_call(x3, lw, tabs3, st)
    del dkb, dvb, km, vm

    def per_tok(a):
        return a.reshape(ns, 1, -1)

    d_out, m_out = _decode_call(layer, page_table, caches, per_tok(dqb), per_tok(qm), per_tok(dk), per_tok(dv),
                                per_tok(ckv), per_tok(kpe), lw, min(pps, page_table.shape[1]))
    x1, h2, q = _post_call(x.reshape(ns, -1), y.reshape(ns, -1), d_out.reshape(ns, -1), m_out.reshape(ns, -1), lw)
    xn = _peer_and_residual(x1, h2, q, lw, gfin, final)
    newconv = jnp.stack([state[:, 1, :], z.reshape(ns, -1)], axis=1)
    return xn.reshape(ns, 1, -1), (newconv, per_tok(dk), per_tok(dv), per_tok(ckv), per_tok(kpe))


def kernel(x_prompt, x_sample, state_conv, cache_diff_k, cache_diff_v, cache_mla_ckv, cache_mla_kpe,
           page_table, ln1_g, ln2_g, w_in, conv_w, diff_lambda, diff_norm_g, mla_q_norm_g, mla_kv_norm_g,
           mla_w_uq, mla_w_uk, mla_w_uv, w_out, peer_w_q, peer_sub_keys, peer_u, peer_v, final_norm_g):
    params = dict(ln1_g=ln1_g, ln2_g=ln2_g, w_in=w_in, conv_w=conv_w, diff_lambda=diff_lambda,
                  diff_norm_g=diff_norm_g, mla_q_norm_g=mla_q_norm_g, mla_kv_norm_g=mla_kv_norm_g,
                  mla_w_uq=mla_w_uq, mla_w_uk=mla_w_uk, mla_w_uv=mla_w_uv, w_out=w_out,
                  peer_w_q=peer_w_q, peer_sub_keys=peer_sub_keys, peer_u=peer_u, peer_v=peer_v)
    depth = ln1_g.shape[0]
    s = x_prompt.shape[1]
    ns = x_sample.shape[0]
    past_len = page_table.shape[1] * cache_diff_k.shape[2]
    tabs_p = _rope_tabs(jnp.arange(s))
    tabs_s = _rope_tabs(jnp.full((ns,), past_len, I32))
    gfin = final_norm_g.reshape(1, -1)
    caches = (cache_diff_k, cache_diff_v, cache_mla_ckv, cache_mla_kpe)

    xp, xs = x_prompt, x_sample.reshape(ns, -1)
    rows_p, rows_s = [], []
    for l in range(depth):
        lw = _prep_layer(l, params)
        final = l == depth - 1
        xp, new_p = _layer_prompt(xp, lw, tabs_p, gfin, final)
        xs3, new_s = _layer_sample(xs, lw, tabs_s, gfin, final, l, state_conv[l], caches, page_table)
        xs = xs3.reshape(ns, -1)
        rows_p.append(new_p)
        rows_s.append(new_s)
    outs_p = [jnp.stack(r) for r in zip(*rows_p)]
    outs_s = [jnp.stack(r) for r in zip(*rows_s)]
    return (xp, xs.reshape(ns, 1, -1), *outs_p, *outs_s)
```

```python
import functools
import math

import jax
import jax.numpy as jnp
from jax import lax
from jax.experimental import pallas as pl
from jax.experimental.pallas import tpu as pltpu

F32 = jnp.float32
BF16 = jnp.bfloat16
I32 = jnp.int32

D_MODEL = 1024
CONV_W = 256
CONV_K = 3
DIFF_HEADS = 4
DIFF_D = 32
DIFF_V = 64
DIFF_QK = 256
DIFF_W = 256
MLA_HEADS = 8
MLA_NOPE = 64
MLA_ROPE = 32
MLA_V = 64
MLA_KV = 256
MLA_Q = 384
MLA_W = 512
MLA_HEAD_PAD = 128
PEER_HEADS = 8
PEER_KEYS = 128
PEER_DKEY = 256
PEER_TOPK = 16
N_EXPERTS = PEER_KEYS * PEER_KEYS
PEER_SEL = PEER_HEADS * PEER_TOPK
ROPE_THETA = 10000.0
NORM_EPS = 1e-6
NEG_INF = -1e30
PAGE = 128
IN_W = 2208
IN_W_EXT = 2304
LANES = 128
SUBLANES = 8
VMEM_LIMIT = 56 * 1024 * 1024

DIFF_SCALE = DIFF_D ** -0.5
MLA_SCALE = (MLA_NOPE + MLA_ROPE) ** -0.5

_NT = (((1,), (1,)), ((), ()))


def _cparams(n_axes):
    return pltpu.CompilerParams(dimension_semantics=("arbitrary",) * n_axes,
                                vmem_limit_bytes=VMEM_LIMIT)


def _rms(x, g):
    ms = jnp.mean(x * x, axis=-1, keepdims=True)
    return x * lax.rsqrt(ms + NORM_EPS) * g


def _dot(a, b):
    return jnp.dot(a, b, preferred_element_type=F32)


def _dot_nt(a, b):
    return lax.dot_general(a, b, _NT, preferred_element_type=F32)


def _full_spec(shape):
    nd = len(shape)
    return pl.BlockSpec(shape, lambda *_: (0,) * nd)


def _proj_body(decode, tb, *refs):
    (x_ref, g1_ref, win_ref, cw_ref, qg_ref, kvg_ref, wq_ref, wqr_ref, wk_ref, wv_ref,
     cq_ref, ck_ref, sn_ref) = refs[:13]
    rest = refs[13:]
    if decode:
        s0_ref, s1_ref = rest[:2]
        rest = rest[2:]
    (y_ref, dqb_ref, dk_ref, dkb_ref, dv_ref, dvb_ref, ckv_ref, kpe_ref,
     qm_ref, km_ref, vm_ref, zo_ref) = rest[:12]

    x = x_ref[0]
    h = _rms(x, g1_ref[...]).astype(BF16)

    def proj(a, b):
        return _dot(h, win_ref[:, a:b])

    bg = proj(0, 256)
    z = proj(256, 512) * proj(512, 768)
    w0, w1, w2 = cw_ref[0:1, :], cw_ref[1:2, :], cw_ref[2:3, :]
    if decode:
        y = bg * (w0 * s0_ref[0] + w1 * s1_ref[0] + w2 * z)
        zo_ref[0] = z
    else:
        zs_ref = rest[12]
        j = pl.program_id(1)

        @pl.when(j == 0)
        def _():
            zs_ref[0:SUBLANES, :] = jnp.zeros((SUBLANES, CONV_W), F32)

        zs_ref[SUBLANES:SUBLANES + tb, :] = z
        zm1 = zs_ref[SUBLANES - 1:SUBLANES - 1 + tb, :]
        zm2 = zs_ref[SUBLANES - 2:SUBLANES - 2 + tb, :]
        y = bg * (w0 * zm2 + w1 * zm1 + w2 * z)
        zs_ref[0:SUBLANES, :] = z[tb - SUBLANES:tb, :]
        zo_ref[0] = z[tb - 2:tb, :]
    y_ref[0] = y.astype(BF16)

    dqb_ref[0] = (proj(768, 1024) * DIFF_SCALE).astype(BF16)
    dk = proj(1024, 1280)
    dk_ref[0] = dk
    dkb_ref[0] = dk.astype(BF16)
    dv = proj(1280, 1536)
    dv_ref[0] = dv
    dvb_ref[0] = dv.astype(BF16)

    cqn = _rms(proj(1536, 1920), qg_ref[...]).astype(BF16)
    cq8 = jnp.concatenate([cq_ref[0]] * MLA_HEADS, axis=1)
    sn8 = jnp.concatenate([sn_ref[0]] * MLA_HEADS, axis=1)
    qm = (_dot(cqn, wq_ref[...]) * cq8 + _dot(cqn, wqr_ref[...]) * sn8) * MLA_SCALE
    qm_ref[0] = qm.astype(BF16)

    ckvn = _rms(proj(1920, 2176), kvg_ref[...])
    ckv_ref[0] = ckvn
    ckb = ckvn.astype(BF16)
    tl = proj(2176, 2304)
    kr = tl * ck_ref[0] + pltpu.roll(tl, LANES - MLA_ROPE, 1) * sn_ref[0]
    kpe_ref[0] = kr[:, MLA_NOPE:MLA_NOPE + MLA_ROPE]
    km = _dot(ckb, wk_ref[...]) + jnp.concatenate([kr] * MLA_HEADS, axis=1)
    km_ref[0] = km.astype(BF16)
    vm_ref[0] = _dot(ckb, wv_ref[...]).astype(BF16)


def _proj_call(x3, lw, tabs, state=None):
    b, s, _ = x3.shape
    decode = state is not None
    tb = min(s, 512)
    assert s % tb == 0
    grid = (b, s // tb)

    def tok(w):
        return pl.BlockSpec((1, tb, w), lambda i, j: (i, j, 0))

    weights = [lw["g1"], lw["w_in"], lw["conv_w"], lw["qg"], lw["kvg"], lw["wq"], lw["wqr"], lw["wk"], lw["wv"]]
    tab = pl.BlockSpec((1, tb, LANES), lambda i, j: (0, j, 0))
    in_specs = [tok(D_MODEL)] + [_full_spec(w.shape) for w in weights] + [tab] * 3
    args = [x3] + weights + list(tabs)
    if decode:
        in_specs += [tok(CONV_W)] * 2
        args += list(state)

    def sds(w, dt):
        return jax.ShapeDtypeStruct((b, s, w), dt)

    out_shape = [sds(CONV_W, BF16), sds(DIFF_QK, BF16), sds(DIFF_QK, F32), sds(DIFF_QK, BF16),
                 sds(DIFF_W, F32), sds(DIFF_W, BF16), sds(MLA_KV, F32), sds(MLA_ROPE, F32),
                 sds(MLA_HEADS * MLA_HEAD_PAD, BF16), sds(MLA_HEADS * MLA_HEAD_PAD, BF16), sds(MLA_W, BF16)]
    out_specs = [tok(CONV_W), tok(DIFF_QK), tok(DIFF_QK), tok(DIFF_QK), tok(DIFF_W), tok(DIFF_W),
                 tok(MLA_KV), tok(MLA_ROPE), tok(MLA_HEADS * MLA_HEAD_PAD), tok(MLA_HEADS * MLA_HEAD_PAD),
                 tok(MLA_W)]
    scratch = []
    if decode:
        out_shape.append(sds(CONV_W, F32))
        out_specs.append(tok(CONV_W))
    else:
        out_shape.append(jax.ShapeDtypeStruct((b, CONV_K - 1, CONV_W), F32))
        out_specs.append(pl.BlockSpec((1, CONV_K - 1, CONV_W), lambda i, j: (i, 0, 0)))
        scratch.append(pltpu.VMEM((tb + SUBLANES, CONV_W), F32))
    return pl.pallas_call(
        functools.partial(_proj_body, decode, tb),
        grid=grid, in_specs=in_specs, out_specs=out_specs, out_shape=out_shape,
        scratch_shapes=scratch, compiler_params=_cparams(2), name="proj",
    )(*args)


def _flash_update(s, v, mask, m_ref, l_ref, acc_ref):
    if mask is not None:
        s = jnp.where(mask, s, NEG_INF)
    tiles = [s[:, c * LANES:(c + 1) * LANES] for c in range(s.shape[1] // LANES)]
    m_prev = m_ref[...]
    row_max = jnp.max(functools.reduce(jnp.maximum, tiles), axis=-1, keepdims=True)
    m_new = jnp.maximum(m_prev, row_max)
    alpha = jnp.exp(m_prev - m_new)
    p_tiles = [jnp.exp(t - m_new) for t in tiles]
    l_ref[...] = alpha * l_ref[...] + functools.reduce(jnp.add, p_tiles)
    p = jnp.concatenate(p_tiles, axis=1).astype(BF16)
    acc_ref[...] = alpha * acc_ref[...] + _dot(p, v)
    m_ref[...] = m_new


def _init_state(m_ref, l_ref, acc_ref):
    m_ref[...] = jnp.full(m_ref.shape, NEG_INF, F32)
    l_ref[...] = jnp.zeros(l_ref.shape, F32)
    acc_ref[...] = jnp.zeros(acc_ref.shape, F32)


def _causal_mask(rows, tq, tk, i, j):
    row = lax.broadcasted_iota(I32, (rows, tk), 0) & (tq - 1)
    col = lax.broadcasted_iota(I32, (rows, tk), 1)
    return col + j * tk <= row + i * tq


def _causal_sweep(tq, tk, i, scores, update, s_a, s_b):
    assert tk % tq == 0 and tq & (tq - 1) == 0
    j_diag = i // (tk // tq)
    n_pairs = j_diag // 2
    scores(0, s_a)

    def body(jj, c):
        j = 2 * jj
        scores(j + 1, s_b)
        update(s_a, j, False)
        scores(j + 2, s_a)
        update(s_b, j + 1, False)
        return c

    lax.fori_loop(0, n_pairs, body, 0)
    j0 = 2 * n_pairs

    @pl.when(j_diag == j0)
    def _():
        update(s_a, j0, True)

    @pl.when(j_diag != j0)
    def _():
        scores(j0 + 1, s_b)
        update(s_a, j0, False)
        update(s_b, j0 + 1, True)


def _diff_attn_body(tq, tk, lam_ref, q_ref, k_ref, v_ref, g_ref, o_ref, m_ref, l_ref, acc_ref, s_a, s_b):
    i = pl.program_id(2)
    q = q_ref[0].astype(F32)
    lane = lax.broadcasted_iota(I32, (1, LANES), 1)
    seg = lane >> 5
    qs = jnp.concatenate([jnp.where(seg == g, q, 0.0) for g in range(4)], axis=0).astype(BF16)
    _init_state(m_ref, l_ref, acc_ref)

    def scores(j, s_ref):
        s_ref[...] = _dot_nt(qs, k_ref[0, pl.ds(pl.multiple_of(j * tk, tk), tk), :])

    def update(s_ref, j, masked):
        v = v_ref[0, pl.ds(pl.multiple_of(j * tk, tk), tk), :]
        mask = _causal_mask(4 * tq, tq, tk, i, j) if masked else None
        _flash_update(s_ref[...], v, mask, m_ref, l_ref, acc_ref)

    _causal_sweep(tq, tk, i, scores, update, s_a, s_b)

    lam = lam_ref[0]
    o_all = acc_ref[...] / jnp.sum(l_ref[...], axis=-1, keepdims=True)
    o0 = o_all[0:tq] - lam * o_all[tq:2 * tq]
    o1 = o_all[2 * tq:3 * tq] - lam * o_all[3 * tq:4 * tq]
    first = lane < DIFF_V
    o = jnp.where(first, o0, o1)
    sq = o * o
    ss0 = jnp.sum(jnp.where(first, sq, 0.0), axis=-1, keepdims=True)
    ss1 = jnp.sum(jnp.where(first, 0.0, sq), axis=-1, keepdims=True)
    ms = jnp.where(first, ss0, ss1) * (1.0 / DIFF_V)
    o_ref[0] = (o * lax.rsqrt(ms + NORM_EPS) * g_ref[...]).astype(BF16)


def _diff_attn_call(dq, dk, dv, lam, gain, tq, tk):
    b, s, _ = dq.shape
    n_pairs = DIFF_QK // LANES
    grid = (b, n_pairs, s // tq)
    return pl.pallas_call(
        functools.partial(_diff_attn_body, tq, tk),
        grid=grid,
        in_specs=[pl.BlockSpec(memory_space=pltpu.SMEM),
                  pl.BlockSpec((1, tq, LANES), lambda bi, hp, i: (bi, i, hp)),
                  pl.BlockSpec((1, s, LANES), lambda bi, hp, i: (bi, 0, hp)),
                  pl.BlockSpec((1, s, LANES), lambda bi, hp, i: (bi, 0, hp)),
                  pl.BlockSpec((1, LANES), lambda bi, hp, i: (0, hp))],
        out_specs=pl.BlockSpec((1, tq, LANES), lambda bi, hp, i: (bi, i, hp)),
        out_shape=jax.ShapeDtypeStruct((b, s, DIFF_W), BF16),
        scratch_shapes=[pltpu.VMEM((4 * tq, LANES), F32)] * 3 + [pltpu.VMEM((4 * tq, tk), F32)] * 2,
        compiler_params=_cparams(3), name="diff_attn",
    )(lam, dq, dk, dv, gain)


def _mla_attn_body(tq, tk, q_ref, k_ref, v_ref, o_ref, m_ref, l_ref, acc_ref, s_a, s_b):
    i = pl.program_id(2)
    q = q_ref[0]
    _init_state(m_ref, l_ref, acc_ref)

    def scores(j, s_ref):
        k = k_ref[0, pl.ds(pl.multiple_of(j * tk, tk), tk), :]
        for hh in range(2):
            sl = slice(hh * MLA_HEAD_PAD, (hh + 1) * MLA_HEAD_PAD)
            s_ref[hh] = _dot_nt(q[:, sl], k[:, sl])

    def update(s_ref, j, masked):
        v = v_ref[0, pl.ds(pl.multiple_of(j * tk, tk), tk), :]
        mask = _causal_mask(tq, tq, tk, i, j) if masked else None
        for hh in range(2):
            _flash_update(s_ref[hh], v, mask, m_ref.at[hh], l_ref.at[hh], acc_ref.at[hh])

    _causal_sweep(tq, tk, i, scores, update, s_a, s_b)
    lane = lax.broadcasted_iota(I32, (1, LANES), 1)
    o = acc_ref[...] / jnp.sum(l_ref[...], axis=-1, keepdims=True)
    o_ref[0] = jnp.where(lane < MLA_V, o[0], o[1]).astype(BF16)


def _mla_attn_call(qm, km, vm, tq, tk):
    b, s, _ = qm.shape
    n_pairs = MLA_HEADS // 2
    grid = (b, n_pairs, s // tq)
    qw = 2 * MLA_HEAD_PAD
    return pl.pallas_call(
        functools.partial(_mla_attn_body, tq, tk),
        grid=grid,
        in_specs=[pl.BlockSpec((1, tq, qw), lambda bi, hp, i: (bi, i, hp)),
                  pl.BlockSpec((1, s, qw), lambda bi, hp, i: (bi, 0, hp)),
                  pl.BlockSpec((1, s, LANES), lambda bi, hp, i: (bi, 0, hp))],
        out_specs=pl.BlockSpec((1, tq, LANES), lambda bi, hp, i: (bi, i, hp)),
        out_shape=jax.ShapeDtypeStruct((b, s, MLA_W), BF16),
        scratch_shapes=[pltpu.VMEM((2, tq, LANES), F32)] * 3 + [pltpu.VMEM((2, tq, tk), F32)] * 2,
        compiler_params=_cparams(3), name="mla_attn",
    )(qm, km, vm)


def _decode_body(sps, pps, *refs):
    (dq_ref, qm_ref, dks_ref, dvs_ref, ckvs_ref, kpes_ref, wk_ref, sel_ref, wuv_ref,
     gain_ref, lam_ref) = refs[1:12]
    n_pg = sps * pps
    pages = refs[12:12 + 4 * n_pg]
    do_ref, mo_ref = refs[12 + 4 * n_pg:14 + 4 * n_pg]
    (qbd_ref, qlat_ref, qpe_ref, md_ref, ld_ref, accd_ref, mm_ref, lm_ref, accm_ref) = refs[14 + 4 * n_pg:]

    p = pl.program_id(1)
    n_steps = pl.num_programs(1)
    n_rows = SUBLANES

    @pl.when(p == 0)
    def _():
        for u in range(sps):
            sub = lax.broadcasted_iota(I32, (n_rows, DIFF_QK), 0)
            lane = lax.broadcasted_iota(I32, (n_rows, DIFF_QK), 1)
            row = jnp.broadcast_to(dq_ref[u].astype(F32), (n_rows, DIFF_QK))
            qbd = jnp.where((lane >> 5) == sub, row, 0.0)
            qbd_ref[u] = qbd.astype(BF16)
            md_ref[u] = jnp.sum(qbd * dks_ref[u], axis=1, keepdims=True)
            ld_ref[u] = jnp.ones((n_rows, 1), F32)
            accd_ref[u] = jnp.broadcast_to(dvs_ref[u], (n_rows, DIFF_W))

            wq = MLA_HEADS * MLA_HEAD_PAD
            sub = lax.broadcasted_iota(I32, (n_rows, wq), 0)
            lane = lax.broadcasted_iota(I32, (n_rows, wq), 1)
            row = jnp.broadcast_to(qm_ref[u].astype(F32), (n_rows, wq))
            qf = jnp.where((lane >> 7) == sub, row, 0.0).astype(BF16)
            qlat = _dot_nt(qf, wk_ref[...]).astype(BF16)
            qpe = _dot(qf, sel_ref[...]).astype(BF16)
            qlat_ref[u] = qlat
            qpe_ref[u] = qpe
            mm_ref[u] = (jnp.sum(qlat.astype(F32) * ckvs_ref[u], axis=1, keepdims=True)
                         + jnp.sum(qpe.astype(F32) * kpes_ref[u], axis=1, keepdims=True))
            lm_ref[u] = jnp.ones((n_rows, 1), F32)
            accm_ref[u] = jnp.broadcast_to(ckvs_ref[u], (n_rows, MLA_KV))

    def cat(rs, axis):
        return jnp.concatenate([r[0, 0] for r in rs], axis=axis).astype(BF16)

    def update(s, v, u, m_ref, l_ref, acc_ref):
        m_prev = m_ref[u]
        m_new = jnp.maximum(m_prev, jnp.max(s, axis=-1, keepdims=True))
        alpha = jnp.exp(m_prev - m_new)
        pr = jnp.exp(s - m_new)
        l_ref[u] = alpha * l_ref[u] + jnp.sum(pr, axis=-1, keepdims=True)
        acc_ref[u] = alpha * acc_ref[u] + _dot(pr.astype(BF16), v)
        m_ref[u] = m_new

    for u in range(sps):
        kd, vd, ckv, kpe = (pages[(c * sps + u) * pps:(c * sps + u + 1) * pps] for c in range(4))
        update(_dot_nt(qbd_ref[u], cat(kd, 0)), cat(vd, 0), u, md_ref, ld_ref, accd_ref)
        c = cat(ckv, 0)
        update(_dot_nt(qlat_ref[u], c) + _dot(qpe_ref[u], cat(kpe, 1)), c, u, mm_ref, lm_ref, accm_ref)

    @pl.when(p == n_steps - 1)
    def _():
        lam = lam_ref[...]
        for u in range(sps):
            sub = lax.broadcasted_iota(I32, (n_rows, DIFF_W), 0)
            lane = lax.broadcasted_iota(I32, (n_rows, DIFF_W), 1)
            coef = jnp.where((sub & 1) == 0, 1.0, -lam)
            o2 = accd_ref[u] / ld_ref[u] * coef
            orow = jnp.sum(jnp.where((lane >> 6) == (sub >> 1), o2, 0.0), axis=0, keepdims=True)
            lane1 = lane[0:1, :] >> 6
            sq = orow * orow
            ms = jnp.zeros_like(orow)
            for g in range(DIFF_HEADS):
                ssg = jnp.sum(jnp.where(lane1 == g, sq, 0.0), axis=-1, keepdims=True)
                ms = jnp.where(lane1 == g, ssg, ms)
            ms = ms * (1.0 / DIFF_V)
            do_ref[u] = (orow * lax.rsqrt(ms + NORM_EPS) * gain_ref[...]).astype(BF16)

            olat = (accm_ref[u] / lm_ref[u]).astype(BF16)
            r = _dot(olat, wuv_ref[...])
            sub = lax.broadcasted_iota(I32, (n_rows, MLA_W), 0)
            lane = lax.broadcasted_iota(I32, (n_rows, MLA_W), 1)
            mo_ref[u] = jnp.sum(jnp.where((lane >> 6) == sub, r, 0.0), axis=0, keepdims=True).astype(BF16)


def _decode_call(layer, page_table, caches, dq, qm, dks, dvs, ckvs, kpes, lw, sps, pps):
    ns, n_pages = page_table.shape
    assert n_pages % pps == 0 and ns % sps == 0
    n_steps = n_pages // pps
    cache_k, cache_v, cache_ckv, cache_kpe = caches
    cache_kpe_t = jnp.swapaxes(cache_kpe, 2, 3)

    def tok(w):
        return pl.BlockSpec((sps, 1, w), lambda bi, p, pt: (bi, 0, 0))

    def full(shape):
        nd = len(shape)
        return pl.BlockSpec(shape, lambda bi, p, pt: (0,) * nd)

    def page(shape, u, r):
        return pl.BlockSpec((1, 1) + shape,
                            lambda bi, p, pt: (layer, pt[(bi * sps + u) * n_pages + p * pps + r], 0, 0))

    def pages(shape):
        return [page(shape, u, r) for u in range(sps) for r in range(pps)]

    weights = [lw["wk"], lw["sel"], lw["wv"], lw["diff_gain"], lw["lam11"]]
    n_pg = sps * pps
    in_specs = ([tok(DIFF_QK), tok(MLA_HEADS * MLA_HEAD_PAD), tok(DIFF_QK), tok(DIFF_W), tok(MLA_KV), tok(MLA_ROPE)]
                + [full(w.shape) for w in weights]
                + pages((PAGE, DIFF_QK)) + pages((PAGE, DIFF_W)) + pages((PAGE, MLA_KV)) + pages((MLA_ROPE, PAGE)))
    args = ([dq, qm, dks, dvs, ckvs, kpes] + weights
            + [cache_k] * n_pg + [cache_v] * n_pg + [cache_ckv] * n_pg + [cache_kpe_t] * n_pg)

    def state(w, dt=F32):
        return pltpu.VMEM((sps, SUBLANES, w), dt)

    grid_spec = pltpu.PrefetchScalarGridSpec(
        num_scalar_prefetch=1, grid=(ns // sps, n_steps), in_specs=in_specs,
        out_specs=[tok(DIFF_W), tok(MLA_W)],
        scratch_shapes=[state(DIFF_QK, BF16), state(MLA_KV, BF16), state(MLA_ROPE, BF16),
                        state(1), state(1), state(DIFF_W), state(1), state(1), state(MLA_KV)])
    return pl.pallas_call(
        functools.partial(_decode_body, sps, pps),
        grid_spec=grid_spec,
        out_shape=[jax.ShapeDtypeStruct((ns, 1, DIFF_W), BF16), jax.ShapeDtypeStruct((ns, 1, MLA_W), BF16)],
        compiler_params=_cparams(2), name="decode_attn",
    )(page_table.reshape(-1), *args)


def _post_body(x_ref, y_ref, d_ref, m_ref, wo_ref, g2_ref, wqh_ref, wql_ref, x1_ref, h2_ref, q_ref):
    x1 = (x_ref[...] + _dot(y_ref[...], wo_ref[0:CONV_W, :])
          + _dot(d_ref[...], wo_ref[CONV_W:CONV_W + DIFF_W, :])
          + _dot(m_ref[...], wo_ref[CONV_W + DIFF_W:, :]))
    x1_ref[...] = x1
    h2 = _rms(x1, g2_ref[...])
    hb = h2.astype(BF16)
    hl = (h2 - hb.astype(F32)).astype(BF16)
    q_ref[...] = _dot(hb, wqh_ref[...]) + _dot(hb, wql_ref[...]) + _dot(hl, wqh_ref[...])
    h2_ref[...] = hb


def _post_call(x, y, d, m, lw):
    t = x.shape[0]
    tb = min(t, 256)
    qw = PEER_HEADS * PEER_DKEY

    def tok(w):
        return pl.BlockSpec((tb, w), lambda i: (i, 0))

    weights = [lw["w_out"], lw["g2"], lw["wq_hi"], lw["wq_lo"]]
    return pl.pallas_call(
        _post_body, grid=(t // tb,),
        in_specs=[tok(D_MODEL), tok(CONV_W), tok(DIFF_W), tok(MLA_W)] + [_full_spec(w.shape) for w in weights],
        out_specs=[tok(D_MODEL), tok(D_MODEL), tok(qw)],
        out_shape=[jax.ShapeDtypeStruct((t, D_MODEL), F32), jax.ShapeDtypeStruct((t, D_MODEL), BF16),
                   jax.ShapeDtypeStruct((t, qw), F32)],
        compiler_params=_cparams(1), name="post_attn",
    )(x, y, d, m, *weights)


_CAND_PAIRS = [(a, b_) for a in range(PEER_TOPK) for b_ in range(PEER_TOPK) if (a + 1) * (b_ + 1) <= PEER_TOPK]
_CAND_ROWS = -(-len(_CAND_PAIRS) // SUBLANES) * SUBLANES


def _cand_tables():
    n_pad = _CAND_ROWS - len(_CAND_PAIRS)
    flat = [a * PEER_TOPK + b_ for a, b_ in _CAND_PAIRS] + [PEER_TOPK * PEER_TOPK] * n_pad
    pen = [0.0] * len(_CAND_PAIRS) + [-math.inf] * n_pad
    rows = jnp.arange(PEER_TOPK)[None, :]
    pick_a = (jnp.asarray([a for a, _ in _CAND_PAIRS] + [0] * n_pad)[:, None] == rows).astype(BF16)
    pick_b = (jnp.asarray([b_ for _, b_ in _CAND_PAIRS] + [0] * n_pad)[:, None] == rows).astype(BF16)
    return (jnp.asarray(flat, I32).reshape(-1, 1), jnp.asarray(pen, F32).reshape(-1, 1), pick_a, pick_b)


def _pick_rows(onehot, x):
    hi = x.astype(BF16)
    r1 = x - hi.astype(F32)
    mid = r1.astype(BF16)
    lo = (r1 - mid.astype(F32)).astype(BF16)
    return (_dot(onehot, hi) + _dot(onehot, mid)) + _dot(onehot, lo)


def _route_body(tb, q_ref, khi_ref, klo_ref, flat_ref, pen_ref, pa_ref, pb_ref, io_ref, jo_ref, go_ref,
                sv_ref, si_ref, i_scr, j_scr, g_scr):
    h = pl.program_id(1)
    q = q_ref[...]
    qh = q.astype(BF16)
    ql = (q - qh.astype(F32)).astype(BF16)
    n_iota = lax.broadcasted_iota(I32, (PEER_KEYS, tb), 0)

    def sub_scores(p):
        sl = slice(p * PEER_KEYS, (p + 1) * PEER_KEYS)
        kh, kl = khi_ref[0, p], klo_ref[0, p]
        return _dot_nt(kh, qh[:, sl]) + _dot_nt(kh, ql[:, sl]) + _dot_nt(kl, qh[:, sl])

    def it(k, ss):
        out = []
        for p, s in enumerate(ss):
            m = jnp.max(s, axis=0, keepdims=True)
            idx = jnp.min(jnp.where(s == m, n_iota, PEER_KEYS), axis=0, keepdims=True)
            sv_ref[p, pl.ds(k, 1), :] = m
            si_ref[p, pl.ds(k, 1), :] = idx
            out.append(jnp.where(n_iota == idx, -jnp.inf, s))
        return tuple(out)

    lax.fori_loop(0, PEER_TOPK, it, (sub_scores(0), sub_scores(1)))

    sv0, sv1 = sv_ref[0], sv_ref[1]
    si0, si1 = si_ref[0], si_ref[1]
    cand0 = (_pick_rows(pa_ref[...], sv0) + _pick_rows(pb_ref[...], sv1)) + pen_ref[...]
    flat = jnp.broadcast_to(flat_ref[...], (_CAND_ROWS, tb))
    k_iota = lax.broadcasted_iota(I32, (PEER_TOPK, tb), 0)
    base = pl.multiple_of(h * PEER_TOPK, PEER_TOPK)

    def it2(k, cand):
        m = jnp.max(cand, axis=0, keepdims=True)
        fl = jnp.min(jnp.where(cand == m, flat, PEER_TOPK * PEER_TOPK), axis=0, keepdims=True)
        a = fl >> 4
        b_ = fl & (PEER_TOPK - 1)
        g_scr[pl.ds(base + k, 1), :] = m
        i_scr[pl.ds(base + k, 1), :] = jnp.sum(jnp.where(k_iota == a, si0, 0), axis=0, keepdims=True)
        j_scr[pl.ds(base + k, 1), :] = jnp.sum(jnp.where(k_iota == b_, si1, 0), axis=0, keepdims=True)
        return jnp.where(flat == fl, -jnp.inf, cand)

    lax.fori_loop(0, PEER_TOPK, it2, cand0)
    ts = g_scr[pl.ds(base, PEER_TOPK), :]
    e = jnp.exp(ts - ts[0:1, :])
    g_scr[pl.ds(base, PEER_TOPK), :] = e / jnp.sum(e, axis=0, keepdims=True)

    @pl.when(h == PEER_HEADS - 1)
    def _():
        io_ref[...] = i_scr[...].T
        jo_ref[...] = j_scr[...].T
        go_ref[...] = g_scr[...].T


def _route_call(q, lw):
    t = q.shape[0]
    tb = min(t, 256)
    tables = _cand_tables()
    tokq = pl.BlockSpec((tb, PEER_DKEY), lambda i, h: (i, h))
    keys = pl.BlockSpec((1, 2, PEER_KEYS, PEER_DKEY // 2), lambda i, h: (h, 0, 0, 0))
    out = pl.BlockSpec((tb, PEER_SEL), lambda i, h: (i, 0))
    return pl.pallas_call(
        functools.partial(_route_body, tb), grid=(t // tb, PEER_HEADS),
        in_specs=[tokq, keys, keys] + [pl.BlockSpec(tb_.shape, lambda i, h: (0, 0)) for tb_ in tables],
        out_specs=[out, out, out],
        out_shape=[jax.ShapeDtypeStruct((t, PEER_SEL), I32), jax.ShapeDtypeStruct((t, PEER_SEL), I32),
                   jax.ShapeDtypeStruct((t, PEER_SEL), F32)],
        scratch_shapes=[pltpu.VMEM((2, PEER_TOPK, tb), F32), pltpu.VMEM((2, PEER_TOPK, tb), I32),
                        pltpu.VMEM((PEER_SEL, tb), I32), pltpu.VMEM((PEER_SEL, tb), I32),
                        pltpu.VMEM((PEER_SEL, tb), F32)],
        compiler_params=_cparams(2), name="peer_route",
    )(q, lw["keys_hi"], lw["keys_lo"], *tables)


_CHUNK_ROWS = 16
_CHUNK = _CHUNK_ROWS * PEER_KEYS


def _up_body(h_ref, u_ref, i_ref, j_ref, g_ref, o_ref, val_ref):
    c = pl.program_id(1)

    @pl.when(c == 0)
    def _():
        val_ref[...] = jnp.zeros(val_ref.shape, F32)

    a = _dot_nt(h_ref[...], u_ref[...])
    iv, jv = i_ref[...], j_ref[...]
    acc = val_ref[...]
    for r in range(_CHUNK_ROWS):
        got = jnp.take_along_axis(a[:, r * PEER_KEYS:(r + 1) * PEER_KEYS], jv, axis=1)
        acc = acc + jnp.where(iv == c * _CHUNK_ROWS + r, got, 0.0)
    val_ref[...] = acc

    @pl.when(c == pl.num_programs(1) - 1)
    def _():
        o_ref[...] = g_ref[...] * (0.5 * acc * (1.0 + lax.erf(acc * (2.0 ** -0.5))))


def _up_call(h2, i_idx, j_idx, gate, lw):
    t = h2.shape[0]
    tb = min(t, 512)
    sel = pl.BlockSpec((tb, PEER_SEL), lambda i, c: (i, 0))
    return pl.pallas_call(
        _up_body, grid=(t // tb, N_EXPERTS // _CHUNK),
        in_specs=[pl.BlockSpec((tb, D_MODEL), lambda i, c: (i, 0)),
                  pl.BlockSpec((_CHUNK, D_MODEL), lambda i, c: (c, 0)), sel, sel, sel],
        out_specs=sel,
        out_shape=jax.ShapeDtypeStruct((t, PEER_SEL), F32),
        scratch_shapes=[pltpu.VMEM((tb, PEER_SEL), F32)],
        compiler_params=_cparams(2), name="peer_up",
    )(h2, lw["u"], i_idx, j_idx, gate)


_Y_PITCH = PEER_KEYS + SUBLANES


def _down_body(final, tb, x_ref, w_ref, i_ref, j_ref, v_ref, gf_ref, o_ref, y_scr, acc_ref):
    c = pl.program_id(1)

    @pl.when(c == 0)
    def _():
        acc_ref[...] = jnp.zeros(acc_ref.shape, F32)
        shape = (PEER_KEYS, PEER_SEL)
        sub = lax.broadcasted_iota(I32, shape, 0)

        def token(t, carry):
            wrow = jnp.broadcast_to(w_ref[pl.ds(t, 1), :], shape)
            irow = jnp.broadcast_to(i_ref[pl.ds(t, 1), :], shape)
            jrow = jnp.broadcast_to(j_ref[pl.ds(t, 1), :], shape)
            lhs = jnp.where(irow == sub, wrow, 0.0).astype(BF16)
            rhs = jnp.where(jrow == sub, 1.0, 0.0).astype(BF16)
            y_scr[pl.ds(pl.multiple_of(t * _Y_PITCH, SUBLANES), PEER_KEYS), :] = _dot_nt(lhs, rhs)
            return carry

        lax.fori_loop(0, tb, token, 0, unroll=8)

    base = c * _CHUNK_ROWS
    lhs = jnp.concatenate([y_scr[pl.ds(base + r, tb, stride=_Y_PITCH), :] for r in range(_CHUNK_ROWS)],
                          axis=1).astype(BF16)
    acc_ref[...] += _dot(lhs, v_ref[...])

    @pl.when(c == pl.num_programs(1) - 1)
    def _():
        xn = x_ref[...] + acc_ref[...]
        if final:
            xn = _rms(xn, gf_ref[...])
        o_ref[...] = xn


def _down_call(x1, wgt, i_idx, j_idx, lw, gfin, final):
    t = x1.shape[0]
    tb = min(t, 256)
    sel = pl.BlockSpec((tb, PEER_SEL), lambda i, c: (i, 0))
    tok = pl.BlockSpec((tb, D_MODEL), lambda i, c: (i, 0))
    return pl.pallas_call(
        functools.partial(_down_body, final, tb), grid=(t // tb, N_EXPERTS // _CHUNK),
        in_specs=[tok, sel, sel, sel, pl.BlockSpec((_CHUNK, D_MODEL), lambda i, c: (c, 0)),
                  pl.BlockSpec((1, D_MODEL), lambda i, c: (0, 0))],
        out_specs=tok,
        out_shape=jax.ShapeDtypeStruct((t, D_MODEL), F32),
        scratch_shapes=[pltpu.VMEM((tb * _Y_PITCH, PEER_KEYS), F32), pltpu.VMEM((tb, D_MODEL), F32)],
        compiler_params=_cparams(2), name="peer_down",
    )(x1, wgt, i_idx, j_idx, lw["v"], gfin)


def _split_bf16(w):
    hi = w.astype(BF16)
    return hi, (w - hi.astype(F32)).astype(BF16)


def _rot_half(w):
    half = MLA_ROPE // 2
    return jnp.concatenate([-w[..., half:], w[..., :half]], axis=-1)


def _prep_layer(l, p):
    w_in = p["w_in"][l]
    kpe_w = w_in[:, IN_W - MLA_ROPE:]
    w_in_ext = jnp.concatenate(
        [w_in[:, :IN_W - MLA_ROPE], jnp.zeros((D_MODEL, MLA_NOPE), F32), kpe_w, _rot_half(kpe_w)], axis=1)
    w_uq = p["mla_w_uq"][l]
    pad = jnp.zeros((MLA_Q, MLA_HEADS, MLA_HEAD_PAD - MLA_NOPE - MLA_ROPE), F32)
    wq = jnp.concatenate([w_uq, pad], axis=-1).reshape(MLA_Q, -1)
    wqr = jnp.concatenate([jnp.zeros((MLA_Q, MLA_HEADS, MLA_NOPE), F32), _rot_half(w_uq[..., MLA_NOPE:]), pad],
                          axis=-1).reshape(MLA_Q, -1)
    w_uk = p["mla_w_uk"][l]
    wk = jnp.concatenate([w_uk, jnp.zeros((MLA_KV, MLA_HEADS, MLA_HEAD_PAD - MLA_NOPE), F32)],
                         axis=-1).reshape(MLA_KV, -1)
    rows = jnp.arange(MLA_HEADS * MLA_HEAD_PAD)
    sel = ((rows[:, None] % MLA_HEAD_PAD) == (MLA_NOPE + jnp.arange(MLA_ROPE))[None, :]).astype(BF16)
    lp = p["diff_lambda"][l].astype(F32)
    lam_init = 0.8 - 0.6 * math.exp(-0.3 * l)
    lam = jnp.exp(jnp.sum(lp[0] * lp[1])) - jnp.exp(jnp.sum(lp[2] * lp[3])) + lam_init
    wq_hi, wq_lo = _split_bf16(p["peer_w_q"][l])
    keys_hi, keys_lo = _split_bf16(p["peer_sub_keys"][l])
    return {
        "g1": p["ln1_g"][l].reshape(1, -1), "g2": p["ln2_g"][l].reshape(1, -1),
        "w_in": w_in_ext.astype(BF16), "conv_w": p["conv_w"][l],
        "qg": p["mla_q_norm_g"][l].reshape(1, -1), "kvg": p["mla_kv_norm_g"][l].reshape(1, -1),
        "wq": wq.astype(BF16), "wqr": wqr.astype(BF16), "wk": wk.astype(BF16),
        "wv": p["mla_w_uv"][l].reshape(MLA_KV, MLA_W).astype(BF16), "sel": sel,
        "lam": lam.reshape(1), "lam11": lam.reshape(1, 1),
        "diff_gain": (jnp.tile(p["diff_norm_g"][l], DIFF_HEADS) * (1.0 - lam_init)).reshape(1, -1),
        "w_out": p["w_out"][l].astype(BF16), "wq_hi": wq_hi, "wq_lo": wq_lo,
        "keys_hi": keys_hi, "keys_lo": keys_lo,
        "u": p["peer_u"][l].astype(BF16), "v": p["peer_v"][l].astype(BF16),
    }


def _rope_tabs(pos):
    half = MLA_ROPE // 2
    inv = ROPE_THETA ** (-jnp.arange(half, dtype=F32) * (2.0 / MLA_ROPE))
    ang = pos.astype(F32)[:, None] * inv[None, :]
    cos2 = jnp.concatenate([jnp.cos(ang)] * 2, axis=-1)
    sin2 = jnp.concatenate([jnp.sin(ang)] * 2, axis=-1)
    n = pos.shape[0]
    z64 = jnp.zeros((n, MLA_NOPE), F32)
    z32 = jnp.zeros((n, LANES - MLA_NOPE - MLA_ROPE), F32)
    cq = jnp.concatenate([jnp.ones((n, MLA_NOPE), F32), cos2, z32], axis=1)
    ck = jnp.concatenate([z64, cos2, z32], axis=1)
    sn = jnp.concatenate([z64, sin2, z32], axis=1)
    return cq, ck, sn


def _peer_and_residual(x1, h2, q, lw, gfin, final):
    i_idx, j_idx, gate = _route_call(q, lw)
    wgt = _up_call(h2, i_idx, j_idx, gate, lw)
    return _down_call(x1, wgt, i_idx, j_idx, lw, gfin, final)


def _layer_prompt(x, lw, tabs, gfin, final, tq_diff=256, tq_mla=512, tk=512):
    b, s, _ = x.shape
    tabs3 = [t[None] for t in tabs]
    (y, dqb, dk, dkb, dv, dvb, ckv, kpe, qm, km, vm, newconv) = _proj_call(x, lw, tabs3)
    tk = min(tk, s)
    d_out = _diff_attn_call(dqb, dkb, dvb, lw["lam"], lw["diff_gain"], min(tq_diff, s), tk)
    m_out = _mla_attn_call(qm, km, vm, min(tq_mla, s), tk)
    t = b * s
    x1, h2, q = _post_call(x.reshape(t, -1), y.reshape(t, -1), d_out.reshape(t, -1), m_out.reshape(t, -1), lw)
    xn = _peer_and_residual(x1, h2, q, lw, gfin, final)
    return xn.reshape(b, s, -1), (newconv, dk, dv, ckv, kpe)


def _layer_sample(x, lw, tabs, gfin, final, layer, state, caches, page_table, sps=2, pps=16):
    ns = x.shape[0]
    x3 = x.reshape(1, ns, -1)
    tabs3 = [t[None] for t in tabs]
    st = (state[:, 0, :][None], state[:, 1, :][None])
    (y, dqb, dk, dkb, dv, dvb, ckv, kpe, qm, km, vm, z) = _proj_call(x3, lw, tabs3, st)
    del dkb, dvb, km, vm

    def per_tok(a):
        return a.reshape(ns, 1, -1)

    d_out, m_out = _decode_call(layer, page_table, caches, per_tok(dqb), per_tok(qm), per_tok(dk), per_tok(dv),
                                per_tok(ckv), per_tok(kpe), lw, sps, min(pps, page_table.shape[1]))
    x1, h2, q = _post_call(x.reshape(ns, -1), y.reshape(ns, -1), d_out.reshape(ns, -1), m_out.reshape(ns, -1), lw)
    xn = _peer_and_residual(x1, h2, q, lw, gfin, final)
    newconv = jnp.stack([state[:, 1, :], z.reshape(ns, -1)], axis=1)
    return xn.reshape(ns, 1, -1), (newconv, per_tok(dk), per_tok(dv), per_tok(ckv), per_tok(kpe))


def kernel(x_prompt, x_sample, state_conv, cache_diff_k, cache_diff_v, cache_mla_ckv, cache_mla_kpe,
           page_table, ln1_g, ln2_g, w_in, conv_w, diff_lambda, diff_norm_g, mla_q_norm_g, mla_kv_norm_g,
           mla_w_uq, mla_w_uk, mla_w_uv, w_out, peer_w_q, peer_sub_keys, peer_u, peer_v, final_norm_g):
    params = dict(ln1_g=ln1_g, ln2_g=ln2_g, w_in=w_in, conv_w=conv_w, diff_lambda=diff_lambda,
                  diff_norm_g=diff_norm_g, mla_q_norm_g=mla_q_norm_g, mla_kv_norm_g=mla_kv_norm_g,
                  mla_w_uq=mla_w_uq, mla_w_uk=mla_w_uk, mla_w_uv=mla_w_uv, w_out=w_out,
                  peer_w_q=peer_w_q, peer_sub_keys=peer_sub_keys, peer_u=peer_u, peer_v=peer_v)
    depth = ln1_g.shape[0]
    s = x_prompt.shape[1]
    ns = x_sample.shape[0]
    past_len = page_table.shape[1] * cache_diff_k.shape[2]
    tabs_p = _rope_tabs(jnp.arange(s))
    tabs_s = _rope_tabs(jnp.full((ns,), past_len, I32))
    gfin = final_norm_g.reshape(1, -1)
    caches = (cache_diff_k, cache_diff_v, cache_mla_ckv, cache_mla_kpe)

    xp, xs = x_prompt, x_sample.reshape(ns, -1)
    rows_p, rows_s = [], []
    for l in range(depth):
        lw = _prep_layer(l, params)
        final = l == depth - 1
        xp, new_p = _layer_prompt(xp, lw, tabs_p, gfin, final)
        xs3, new_s = _layer_sample(xs, lw, tabs_s, gfin, final, l, state_conv[l], caches, page_table)
        xs = xs3.reshape(ns, -1)
        rows_p.append(new_p)
        rows_s.append(new_s)
    outs_p = [jnp.stack(r) for r in zip(*rows_p)]
    outs_s = [jnp.stack(r) for r in zip(*rows_s)]
    return (xp, xs.reshape(ns, 1, -1), *outs_p, *outs_s)
```

```python
import functools
import math

import jax
import jax.numpy as jnp
from jax import lax
from jax.experimental import pallas as pl
from jax.experimental.pallas import tpu as pltpu

F32 = jnp.float32
BF16 = jnp.bfloat16
I32 = jnp.int32

D_MODEL = 1024
CONV_W = 256
CONV_K = 3
DIFF_HEADS = 4
DIFF_D = 32
DIFF_V = 64
DIFF_QK = 256
DIFF_W = 256
MLA_HEADS = 8
MLA_NOPE = 64
MLA_ROPE = 32
MLA_V = 64
MLA_KV = 256
MLA_Q = 384
MLA_W = 512
MLA_HEAD_PAD = 128
PEER_HEADS = 8
PEER_KEYS = 128
PEER_DKEY = 256
PEER_TOPK = 16
N_EXPERTS = PEER_KEYS * PEER_KEYS
PEER_SEL = PEER_HEADS * PEER_TOPK
ROPE_THETA = 10000.0
NORM_EPS = 1e-6
NEG_INF = -1e30
PAGE = 128
IN_W = 2208
IN_W_EXT = 2304
LANES = 128
SUBLANES = 8
VMEM_LIMIT = 56 * 1024 * 1024

DIFF_SCALE = DIFF_D ** -0.5
MLA_SCALE = (MLA_NOPE + MLA_ROPE) ** -0.5

_NT = (((1,), (1,)), ((), ()))


def _cparams(n_axes):
    return pltpu.CompilerParams(dimension_semantics=("arbitrary",) * n_axes,
                                vmem_limit_bytes=VMEM_LIMIT)


def _rms(x, g):
    ms = jnp.mean(x * x, axis=-1, keepdims=True)
    return x * lax.rsqrt(ms + NORM_EPS) * g


def _dot(a, b):
    return jnp.dot(a, b, preferred_element_type=F32)


def _dot_nt(a, b):
    return lax.dot_general(a, b, _NT, preferred_element_type=F32)


def _full_spec(shape):
    nd = len(shape)
    return pl.BlockSpec(shape, lambda *_: (0,) * nd)


def _proj_body(decode, tb, *refs):
    (x_ref, g1_ref, win_ref, cw_ref, qg_ref, kvg_ref, wq_ref, wqr_ref, wk_ref, wv_ref,
     cq_ref, ck_ref, sn_ref) = refs[:13]
    rest = refs[13:]
    if decode:
        s0_ref, s1_ref = rest[:2]
        rest = rest[2:]
    (y_ref, dqb_ref, dk_ref, dkb_ref, dv_ref, dvb_ref, ckv_ref, kpe_ref,
     qm_ref, km_ref, vm_ref, zo_ref) = rest[:12]

    x = x_ref[0]
    h = _rms(x, g1_ref[...]).astype(BF16)

    def proj(a, b):
        return _dot(h, win_ref[:, a:b])

    bg = proj(0, 256)
    z = proj(256, 512) * proj(512, 768)
    w0, w1, w2 = cw_ref[0:1, :], cw_ref[1:2, :], cw_ref[2:3, :]
    if decode:
        y = bg * (w0 * s0_ref[0] + w1 * s1_ref[0] + w2 * z)
        zo_ref[0] = z
    else:
        zs_ref = rest[12]
        j = pl.program_id(1)

        @pl.when(j == 0)
        def _():
            zs_ref[0:SUBLANES, :] = jnp.zeros((SUBLANES, CONV_W), F32)

        zs_ref[SUBLANES:SUBLANES + tb, :] = z
        zm1 = zs_ref[SUBLANES - 1:SUBLANES - 1 + tb, :]
        zm2 = zs_ref[SUBLANES - 2:SUBLANES - 2 + tb, :]
        y = bg * (w0 * zm2 + w1 * zm1 + w2 * z)
        zs_ref[0:SUBLANES, :] = z[tb - SUBLANES:tb, :]
        zo_ref[0] = z[tb - 2:tb, :]
    y_ref[0] = y.astype(BF16)

    dqb_ref[0] = (proj(768, 1024) * DIFF_SCALE).astype(BF16)
    dk = proj(1024, 1280)
    dk_ref[0] = dk
    dkb_ref[0] = dk.astype(BF16)
    dv = proj(1280, 1536)
    dv_ref[0] = dv
    dvb_ref[0] = dv.astype(BF16)

    cqn = _rms(proj(1536, 1920), qg_ref[...]).astype(BF16)
    cq8 = jnp.concatenate([cq_ref[0]] * MLA_HEADS, axis=1)
    sn8 = jnp.concatenate([sn_ref[0]] * MLA_HEADS, axis=1)
    qm = (_dot(cqn, wq_ref[...]) * cq8 + _dot(cqn, wqr_ref[...]) * sn8) * MLA_SCALE
    qm_ref[0] = qm.astype(BF16)

    ckvn = _rms(proj(1920, 2176), kvg_ref[...])
    ckv_ref[0] = ckvn
    ckb = ckvn.astype(BF16)
    tl = proj(2176, 2304)
    kr = tl * ck_ref[0] + pltpu.roll(tl, LANES - MLA_ROPE, 1) * sn_ref[0]
    kpe_ref[0] = kr[:, MLA_NOPE:MLA_NOPE + MLA_ROPE]
    km = _dot(ckb, wk_ref[...]) + jnp.concatenate([kr] * MLA_HEADS, axis=1)
    km_ref[0] = km.astype(BF16)
    vm_ref[0] = _dot(ckb, wv_ref[...]).astype(BF16)


def _proj_call(x3, lw, tabs, state=None):
    b, s, _ = x3.shape
    decode = state is not None
    tb = min(s, 512)
    assert s % tb == 0
    grid = (b, s // tb)

    def tok(w):
        return pl.BlockSpec((1, tb, w), lambda i, j: (i, j, 0))

    weights = [lw["g1"], lw["w_in"], lw["conv_w"], lw["qg"], lw["kvg"], lw["wq"], lw["wqr"], lw["wk"], lw["wv"]]
    tab = pl.BlockSpec((1, tb, LANES), lambda i, j: (0, j, 0))
    in_specs = [tok(D_MODEL)] + [_full_spec(w.shape) for w in weights] + [tab] * 3
    args = [x3] + weights + list(tabs)
    if decode:
        in_specs += [tok(CONV_W)] * 2
        args += list(state)

    def sds(w, dt):
        return jax.ShapeDtypeStruct((b, s, w), dt)

    out_shape = [sds(CONV_W, BF16), sds(DIFF_QK, BF16), sds(DIFF_QK, F32), sds(DIFF_QK, BF16),
                 sds(DIFF_W, F32), sds(DIFF_W, BF16), sds(MLA_KV, F32), sds(MLA_ROPE, F32),
                 sds(MLA_HEADS * MLA_HEAD_PAD, BF16), sds(MLA_HEADS * MLA_HEAD_PAD, BF16), sds(MLA_W, BF16)]
    out_specs = [tok(CONV_W), tok(DIFF_QK), tok(DIFF_QK), tok(DIFF_QK), tok(DIFF_W), tok(DIFF_W),
                 tok(MLA_KV), tok(MLA_ROPE), tok(MLA_HEADS * MLA_HEAD_PAD), tok(MLA_HEADS * MLA_HEAD_PAD),
                 tok(MLA_W)]
    scratch = []
    if decode:
        out_shape.append(sds(CONV_W, F32))
        out_specs.append(tok(CONV_W))
    else:
        out_shape.append(jax.ShapeDtypeStruct((b, CONV_K - 1, CONV_W), F32))
        out_specs.append(pl.BlockSpec((1, CONV_K - 1, CONV_W), lambda i, j: (i, 0, 0)))
        scratch.append(pltpu.VMEM((tb + SUBLANES, CONV_W), F32))
    return pl.pallas_call(
        functools.partial(_proj_body, decode, tb),
        grid=grid, in_specs=in_specs, out_specs=out_specs, out_shape=out_shape,
        scratch_shapes=scratch, compiler_params=_cparams(2), name="proj",
    )(*args)


def _flash_update(s, v, mask, m_ref, l_ref, acc_ref):
    if mask is not None:
        s = jnp.where(mask, s, NEG_INF)
    tiles = [s[:, c * LANES:(c + 1) * LANES] for c in range(s.shape[1] // LANES)]
    m_prev = m_ref[...]
    row_max = jnp.max(functools.reduce(jnp.maximum, tiles), axis=-1, keepdims=True)
    m_new = jnp.maximum(m_prev, row_max)
    alpha = jnp.exp(m_prev - m_new)
    p_tiles = [jnp.exp(t - m_new) for t in tiles]
    l_ref[...] = alpha * l_ref[...] + functools.reduce(jnp.add, p_tiles)
    p = jnp.concatenate(p_tiles, axis=1).astype(BF16)
    acc_ref[...] = alpha * acc_ref[...] + _dot(p, v)
    m_ref[...] = m_new


def _init_state(m_ref, l_ref, acc_ref):
    m_ref[...] = jnp.full(m_ref.shape, NEG_INF, F32)
    l_ref[...] = jnp.zeros(l_ref.shape, F32)
    acc_ref[...] = jnp.zeros(acc_ref.shape, F32)


def _causal_mask(rows, tq, tk, i, j):
    row = lax.broadcasted_iota(I32, (rows, tk), 0) & (tq - 1)
    col = lax.broadcasted_iota(I32, (rows, tk), 1)
    return col + j * tk <= row + i * tq


def _causal_sweep(tq, tk, i, scores, update, s_a, s_b):
    assert tk % tq == 0 and tq & (tq - 1) == 0
    j_diag = i // (tk // tq)
    n_pairs = j_diag // 2
    scores(0, s_a)

    def body(jj, c):
        j = 2 * jj
        scores(j + 1, s_b)
        update(s_a, j, False)
        scores(j + 2, s_a)
        update(s_b, j + 1, False)
        return c

    lax.fori_loop(0, n_pairs, body, 0)
    j0 = 2 * n_pairs

    @pl.when(j_diag == j0)
    def _():
        update(s_a, j0, True)

    @pl.when(j_diag != j0)
    def _():
        scores(j0 + 1, s_b)
        update(s_a, j0, False)
        update(s_b, j0 + 1, True)


def _diff_attn_body(tq, tk, lam_ref, q_ref, k_ref, v_ref, g_ref, o_ref, m_ref, l_ref, acc_ref, s_a, s_b):
    i = pl.program_id(2)
    q = q_ref[0].astype(F32)
    lane = lax.broadcasted_iota(I32, (1, LANES), 1)
    seg = lane >> 5
    qs = jnp.concatenate([jnp.where(seg == g, q, 0.0) for g in range(4)], axis=0).astype(BF16)
    _init_state(m_ref, l_ref, acc_ref)

    def scores(j, s_ref):
        s_ref[...] = _dot_nt(qs, k_ref[0, pl.ds(pl.multiple_of(j * tk, tk), tk), :])

    def update(s_ref, j, masked):
        v = v_ref[0, pl.ds(pl.multiple_of(j * tk, tk), tk), :]
        mask = _causal_mask(4 * tq, tq, tk, i, j) if masked else None
        _flash_update(s_ref[...], v, mask, m_ref, l_ref, acc_ref)

    _causal_sweep(tq, tk, i, scores, update, s_a, s_b)

    lam = lam_ref[0]
    o_all = acc_ref[...] / jnp.sum(l_ref[...], axis=-1, keepdims=True)
    o0 = o_all[0:tq] - lam * o_all[tq:2 * tq]
    o1 = o_all[2 * tq:3 * tq] - lam * o_all[3 * tq:4 * tq]
    first = lane < DIFF_V
    o = jnp.where(first, o0, o1)
    sq = o * o
    ss0 = jnp.sum(jnp.where(first, sq, 0.0), axis=-1, keepdims=True)
    ss1 = jnp.sum(jnp.where(first, 0.0, sq), axis=-1, keepdims=True)
    ms = jnp.where(first, ss0, ss1) * (1.0 / DIFF_V)
    o_ref[0] = (o * lax.rsqrt(ms + NORM_EPS) * g_ref[...]).astype(BF16)


def _diff_attn_call(dq, dk, dv, lam, gain, tq, tk):
    b, s, _ = dq.shape
    n_pairs = DIFF_QK // LANES
    grid = (b, n_pairs, s // tq)
    return pl.pallas_call(
        functools.partial(_diff_attn_body, tq, tk),
        grid=grid,
        in_specs=[pl.BlockSpec(memory_space=pltpu.SMEM),
                  pl.BlockSpec((1, tq, LANES), lambda bi, hp, i: (bi, i, hp)),
                  pl.BlockSpec((1, s, LANES), lambda bi, hp, i: (bi, 0, hp)),
                  pl.BlockSpec((1, s, LANES), lambda bi, hp, i: (bi, 0, hp)),
                  pl.BlockSpec((1, LANES), lambda bi, hp, i: (0, hp))],
        out_specs=pl.BlockSpec((1, tq, LANES), lambda bi, hp, i: (bi, i, hp)),
        out_shape=jax.ShapeDtypeStruct((b, s, DIFF_W), BF16),
        scratch_shapes=[pltpu.VMEM((4 * tq, LANES), F32)] * 3 + [pltpu.VMEM((4 * tq, tk), F32)] * 2,
        compiler_params=_cparams(3), name="diff_attn",
    )(lam, dq, dk, dv, gain)


def _mla_attn_body(tq, tk, q_ref, k_ref, v_ref, o_ref, m_ref, l_ref, acc_ref, s_a, s_b):
    i = pl.program_id(2)
    q = q_ref[0]
    _init_state(m_ref, l_ref, acc_ref)

    def scores(j, s_ref):
        k = k_ref[0, pl.ds(pl.multiple_of(j * tk, tk), tk), :]
        for hh in range(2):
            sl = slice(hh * MLA_HEAD_PAD, (hh + 1) * MLA_HEAD_PAD)
            s_ref[hh] = _dot_nt(q[:, sl], k[:, sl])

    def update(s_ref, j, masked):
        v = v_ref[0, pl.ds(pl.multiple_of(j * tk, tk), tk), :]
        mask = _causal_mask(tq, tq, tk, i, j) if masked else None
        for hh in range(2):
            _flash_update(s_ref[hh], v, mask, m_ref.at[hh], l_ref.at[hh], acc_ref.at[hh])

    _causal_sweep(tq, tk, i, scores, update, s_a, s_b)
    lane = lax.broadcasted_iota(I32, (1, LANES), 1)
    o = acc_ref[...] / jnp.sum(l_ref[...], axis=-1, keepdims=True)
    o_ref[0] = jnp.where(lane < MLA_V, o[0], o[1]).astype(BF16)


def _mla_attn_call(qm, km, vm, tq, tk):
    b, s, _ = qm.shape
    n_pairs = MLA_HEADS // 2
    grid = (b, n_pairs, s // tq)
    qw = 2 * MLA_HEAD_PAD
    return pl.pallas_call(
        functools.partial(_mla_attn_body, tq, tk),
        grid=grid,
        in_specs=[pl.BlockSpec((1, tq, qw), lambda bi, hp, i: (bi, i, hp)),
                  pl.BlockSpec((1, s, qw), lambda bi, hp, i: (bi, 0, hp)),
                  pl.BlockSpec((1, s, LANES), lambda bi, hp, i: (bi, 0, hp))],
        out_specs=pl.BlockSpec((1, tq, LANES), lambda bi, hp, i: (bi, i, hp)),
        out_shape=jax.ShapeDtypeStruct((b, s, MLA_W), BF16),
        scratch_shapes=[pltpu.VMEM((2, tq, LANES), F32)] * 3 + [pltpu.VMEM((2, tq, tk), F32)] * 2,
        compiler_params=_cparams(3), name="mla_attn",
    )(qm, km, vm)


def _decode_body(sps, pps, *refs):
    (dq_ref, qm_ref, dks_ref, dvs_ref, ckvs_ref, kpes_ref, wk_ref, sel_ref, wuv_ref,
     gain_ref, lam_ref) = refs[1:12]
    n_pg = sps * pps
    pages = refs[12:12 + 4 * n_pg]
    do_ref, mo_ref = refs[12 + 4 * n_pg:14 + 4 * n_pg]
    (qbd_ref, qlat_ref, qpe_ref, md_ref, ld_ref, accd_ref, mm_ref, lm_ref, accm_ref) = refs[14 + 4 * n_pg:]

    p = pl.program_id(1)
    n_steps = pl.num_programs(1)
    n_rows = SUBLANES

    @pl.when(p == 0)
    def _():
        for u in range(sps):
            sub = lax.broadcasted_iota(I32, (n_rows, DIFF_QK), 0)
            lane = lax.broadcasted_iota(I32, (n_rows, DIFF_QK), 1)
            row = jnp.broadcast_to(dq_ref[u].astype(F32), (n_rows, DIFF_QK))
            qbd = jnp.where((lane >> 5) == sub, row, 0.0)
            qbd_ref[u] = qbd.astype(BF16)
            md_ref[u] = jnp.sum(qbd * dks_ref[u], axis=1, keepdims=True)
            ld_ref[u] = jnp.ones((n_rows, 1), F32)
            accd_ref[u] = jnp.broadcast_to(dvs_ref[u], (n_rows, DIFF_W))

            wq = MLA_HEADS * MLA_HEAD_PAD
            sub = lax.broadcasted_iota(I32, (n_rows, wq), 0)
            lane = lax.broadcasted_iota(I32, (n_rows, wq), 1)
            row = jnp.broadcast_to(qm_ref[u].astype(F32), (n_rows, wq))
            qf = jnp.where((lane >> 7) == sub, row, 0.0).astype(BF16)
            qlat = _dot_nt(qf, wk_ref[...]).astype(BF16)
            qpe = _dot(qf, sel_ref[...]).astype(BF16)
            qlat_ref[u] = qlat
            qpe_ref[u] = qpe
            mm_ref[u] = (jnp.sum(qlat.astype(F32) * ckvs_ref[u], axis=1, keepdims=True)
                         + jnp.sum(qpe.astype(F32) * kpes_ref[u], axis=1, keepdims=True))
            lm_ref[u] = jnp.ones((n_rows, 1), F32)
            accm_ref[u] = jnp.broadcast_to(ckvs_ref[u], (n_rows, MLA_KV))

    def cat(rs, axis):
        return jnp.concatenate([r[0, 0] for r in rs], axis=axis).astype(BF16)

    def update(s, v, u, m_ref, l_ref, acc_ref):
        m_prev = m_ref[u]
        m_new = jnp.maximum(m_prev, jnp.max(s, axis=-1, keepdims=True))
        alpha = jnp.exp(m_prev - m_new)
        pr = jnp.exp(s - m_new)
        l_ref[u] = alpha * l_ref[u] + jnp.sum(pr, axis=-1, keepdims=True)
        acc_ref[u] = alpha * acc_ref[u] + _dot(pr.astype(BF16), v)
        m_ref[u] = m_new

    for u in range(sps):
        kd, vd, ckv, kpe = (pages[(c * sps + u) * pps:(c * sps + u + 1) * pps] for c in range(4))
        update(_dot_nt(qbd_ref[u], cat(kd, 0)), cat(vd, 0), u, md_ref, ld_ref, accd_ref)
        c = cat(ckv, 0)
        update(_dot_nt(qlat_ref[u], c) + _dot(qpe_ref[u], cat(kpe, 1)), c, u, mm_ref, lm_ref, accm_ref)

    @pl.when(p == n_steps - 1)
    def _():
        lam = lam_ref[...]
        for u in range(sps):
            sub = lax.broadcasted_iota(I32, (n_rows, DIFF_W), 0)
            lane = lax.broadcasted_iota(I32, (n_rows, DIFF_W), 1)
            coef = jnp.where((sub & 1) == 0, 1.0, -lam)
            o2 = accd_ref[u] / ld_ref[u] * coef
            orow = jnp.sum(jnp.where((lane >> 6) == (sub >> 1), o2, 0.0), axis=0, keepdims=True)
            lane1 = lane[0:1, :] >> 6
            sq = orow * orow
            ms = jnp.zeros_like(orow)
            for g in range(DIFF_HEADS):
                ssg = jnp.sum(jnp.where(lane1 == g, sq, 0.0), axis=-1, keepdims=True)
                ms = jnp.where(lane1 == g, ssg, ms)
            ms = ms * (1.0 / DIFF_V)
            do_ref[u] = (orow * lax.rsqrt(ms + NORM_EPS) * gain_ref[...]).astype(BF16)

            olat = (accm_ref[u] / lm_ref[u]).astype(BF16)
            r = _dot(olat, wuv_ref[...])
            sub = lax.broadcasted_iota(I32, (n_rows, MLA_W), 0)
            lane = lax.broadcasted_iota(I32, (n_rows, MLA_W), 1)
            mo_ref[u] = jnp.sum(jnp.where((lane >> 6) == sub, r, 0.0), axis=0, keepdims=True).astype(BF16)


def _decode_call(layer, page_table, caches, dq, qm, dks, dvs, ckvs, kpes, lw, sps, pps):
    ns, n_pages = page_table.shape
    assert n_pages % pps == 0 and ns % sps == 0
    n_steps = n_pages // pps
    cache_k, cache_v, cache_ckv, cache_kpe = caches
    cache_kpe_t = jnp.swapaxes(cache_kpe, 2, 3)

    def tok(w):
        return pl.BlockSpec((sps, 1, w), lambda bi, p, pt: (bi, 0, 0))

    def full(shape):
        nd = len(shape)
        return pl.BlockSpec(shape, lambda bi, p, pt: (0,) * nd)

    def page(shape, u, r):
        return pl.BlockSpec((1, 1) + shape,
                            lambda bi, p, pt: (layer, pt[(bi * sps + u) * n_pages + p * pps + r], 0, 0))

    def pages(shape):
        return [page(shape, u, r) for u in range(sps) for r in range(pps)]

    weights = [lw["wk"], lw["sel"], lw["wv"], lw["diff_gain"], lw["lam11"]]
    n_pg = sps * pps
    in_specs = ([tok(DIFF_QK), tok(MLA_HEADS * MLA_HEAD_PAD), tok(DIFF_QK), tok(DIFF_W), tok(MLA_KV), tok(MLA_ROPE)]
                + [full(w.shape) for w in weights]
                + pages((PAGE, DIFF_QK)) + pages((PAGE, DIFF_W)) + pages((PAGE, MLA_KV)) + pages((MLA_ROPE, PAGE)))
    args = ([dq, qm, dks, dvs, ckvs, kpes] + weights
            + [cache_k] * n_pg + [cache_v] * n_pg + [cache_ckv] * n_pg + [cache_kpe_t] * n_pg)

    def state(w, dt=F32):
        return pltpu.VMEM((sps, SUBLANES, w), dt)

    grid_spec = pltpu.PrefetchScalarGridSpec(
        num_scalar_prefetch=1, grid=(ns // sps, n_steps), in_specs=in_specs,
        out_specs=[tok(DIFF_W), tok(MLA_W)],
        scratch_shapes=[state(DIFF_QK, BF16), state(MLA_KV, BF16), state(MLA_ROPE, BF16),
                        state(1), state(1), state(DIFF_W), state(1), state(1), state(MLA_KV)])
    return pl.pallas_call(
        functools.partial(_decode_body, sps, pps),
        grid_spec=grid_spec,
        out_shape=[jax.ShapeDtypeStruct((ns, 1, DIFF_W), BF16), jax.ShapeDtypeStruct((ns, 1, MLA_W), BF16)],
        compiler_params=_cparams(2), name="decode_attn",
    )(page_table.reshape(-1), *args)


def _post_body(x_ref, y_ref, d_ref, m_ref, wo_ref, g2_ref, wqh_ref, wql_ref, x1_ref, h2_ref, q_ref):
    x1 = (x_ref[...] + _dot(y_ref[...], wo_ref[0:CONV_W, :])
          + _dot(d_ref[...], wo_ref[CONV_W:CONV_W + DIFF_W, :])
          + _dot(m_ref[...], wo_ref[CONV_W + DIFF_W:, :]))
    x1_ref[...] = x1
    h2 = _rms(x1, g2_ref[...])
    hb = h2.astype(BF16)
    hl = (h2 - hb.astype(F32)).astype(BF16)
    q_ref[...] = _dot(hb, wqh_ref[...]) + _dot(hb, wql_ref[...]) + _dot(hl, wqh_ref[...])
    h2_ref[...] = hb


def _post_call(x, y, d, m, lw):
    t = x.shape[0]
    tb = min(t, 256)
    qw = PEER_HEADS * PEER_DKEY

    def tok(w):
        return pl.BlockSpec((tb, w), lambda i: (i, 0))

    weights = [lw["w_out"], lw["g2"], lw["wq_hi"], lw["wq_lo"]]
    return pl.pallas_call(
        _post_body, grid=(t // tb,),
        in_specs=[tok(D_MODEL), tok(CONV_W), tok(DIFF_W), tok(MLA_W)] + [_full_spec(w.shape) for w in weights],
        out_specs=[tok(D_MODEL), tok(D_MODEL), tok(qw)],
        out_shape=[jax.ShapeDtypeStruct((t, D_MODEL), F32), jax.ShapeDtypeStruct((t, D_MODEL), BF16),
                   jax.ShapeDtypeStruct((t, qw), F32)],
        compiler_params=_cparams(1), name="post_attn",
    )(x, y, d, m, *weights)


_CAND_PAIRS = [(a, b_) for a in range(PEER_TOPK) for b_ in range(PEER_TOPK) if (a + 1) * (b_ + 1) <= PEER_TOPK]
_CAND_ROWS = -(-len(_CAND_PAIRS) // SUBLANES) * SUBLANES


def _cand_tables():
    n_pad = _CAND_ROWS - len(_CAND_PAIRS)
    flat = [a * PEER_TOPK + b_ for a, b_ in _CAND_PAIRS] + [PEER_TOPK * PEER_TOPK] * n_pad
    pen = [0.0] * len(_CAND_PAIRS) + [-math.inf] * n_pad
    rows = jnp.arange(PEER_TOPK)[None, :]
    pick_a = (jnp.asarray([a for a, _ in _CAND_PAIRS] + [0] * n_pad)[:, None] == rows).astype(BF16)
    pick_b = (jnp.asarray([b_ for _, b_ in _CAND_PAIRS] + [0] * n_pad)[:, None] == rows).astype(BF16)
    return (jnp.asarray(flat, I32).reshape(-1, 1), jnp.asarray(pen, F32).reshape(-1, 1), pick_a, pick_b)


def _pick_rows(onehot, x):
    hi = x.astype(BF16)
    r1 = x - hi.astype(F32)
    mid = r1.astype(BF16)
    lo = (r1 - mid.astype(F32)).astype(BF16)
    return (_dot(onehot, hi) + _dot(onehot, mid)) + _dot(onehot, lo)


_ROUTE_HEADS = 2


def _route_body(tb, q_ref, khi_ref, klo_ref, flat_ref, pen_ref, pa_ref, pb_ref, io_ref, jo_ref, go_ref,
                sv_ref, si_ref, i_scr, j_scr, g_scr):
    hg = pl.program_id(1)
    q = q_ref[...]
    qh = q.astype(BF16)
    ql = (q - qh.astype(F32)).astype(BF16)
    n_iota = lax.broadcasted_iota(I32, (PEER_KEYS, tb), 0)

    for u in range(_ROUTE_HEADS):
        def sub_scores(p, u=u):
            c0 = (2 * u + p) * PEER_KEYS
            kh, kl = khi_ref[u, p], klo_ref[u, p]
            qh_, ql_ = qh[:, c0:c0 + PEER_KEYS], ql[:, c0:c0 + PEER_KEYS]
            return _dot_nt(kh, qh_) + _dot_nt(kh, ql_) + _dot_nt(kl, qh_)

        def it(k, ss, u=u):
            out = []
            for p, s in enumerate(ss):
                m = jnp.max(s, axis=0, keepdims=True)
                idx = jnp.min(jnp.where(s == m, n_iota, PEER_KEYS), axis=0, keepdims=True)
                sv_ref[u, p, pl.ds(k, 1), :] = m
                si_ref[u, p, pl.ds(k, 1), :] = idx
                out.append(jnp.where(n_iota == idx, -jnp.inf, s))
            return tuple(out)

        lax.fori_loop(0, PEER_TOPK, it, (sub_scores(0), sub_scores(1)))

    cands = tuple((_pick_rows(pa_ref[...], sv_ref[u, 0]) + _pick_rows(pb_ref[...], sv_ref[u, 1])) + pen_ref[...]
                  for u in range(_ROUTE_HEADS))
    flat = jnp.broadcast_to(flat_ref[...], (_CAND_ROWS, tb))
    k_iota = lax.broadcasted_iota(I32, (PEER_TOPK, tb), 0)
    base = pl.multiple_of(hg * (_ROUTE_HEADS * PEER_TOPK), _ROUTE_HEADS * PEER_TOPK)

    def it2(k, cs):
        out = []
        for u, cand in enumerate(cs):
            row = base + u * PEER_TOPK + k
            m = jnp.max(cand, axis=0, keepdims=True)
            fl = jnp.min(jnp.where(cand == m, flat, PEER_TOPK * PEER_TOPK), axis=0, keepdims=True)
            a = fl >> 4
            b_ = fl & (PEER_TOPK - 1)
            g_scr[pl.ds(row, 1), :] = m
            i_scr[pl.ds(row, 1), :] = jnp.sum(jnp.where(k_iota == a, si_ref[u, 0], 0), axis=0, keepdims=True)
            j_scr[pl.ds(row, 1), :] = jnp.sum(jnp.where(k_iota == b_, si_ref[u, 1], 0), axis=0, keepdims=True)
            out.append(jnp.where(flat == fl, -jnp.inf, cand))
        return tuple(out)

    lax.fori_loop(0, PEER_TOPK, it2, cands)
    for u in range(_ROUTE_HEADS):
        rows = pl.ds(base + u * PEER_TOPK, PEER_TOPK)
        ts = g_scr[rows, :]
        e = jnp.exp(ts - ts[0:1, :])
        g_scr[rows, :] = e / jnp.sum(e, axis=0, keepdims=True)

    @pl.when(hg == pl.num_programs(1) - 1)
    def _():
        io_ref[...] = i_scr[...].T
        jo_ref[...] = j_scr[...].T
        go_ref[...] = g_scr[...].T


def _route_call(q, lw):
    t = q.shape[0]
    tb = min(t, 256)
    tables = _cand_tables()
    tokq = pl.BlockSpec((tb, _ROUTE_HEADS * PEER_DKEY), lambda i, h: (i, h))
    keys = pl.BlockSpec((_ROUTE_HEADS, 2, PEER_KEYS, PEER_DKEY // 2), lambda i, h: (h, 0, 0, 0))
    out = pl.BlockSpec((tb, PEER_SEL), lambda i, h: (i, 0))
    return pl.pallas_call(
        functools.partial(_route_body, tb), grid=(t // tb, PEER_HEADS // _ROUTE_HEADS),
        in_specs=[tokq, keys, keys] + [pl.BlockSpec(tb_.shape, lambda i, h: (0, 0)) for tb_ in tables],
        out_specs=[out, out, out],
        out_shape=[jax.ShapeDtypeStruct((t, PEER_SEL), I32), jax.ShapeDtypeStruct((t, PEER_SEL), I32),
                   jax.ShapeDtypeStruct((t, PEER_SEL), F32)],
        scratch_shapes=[pltpu.VMEM((_ROUTE_HEADS, 2, PEER_TOPK, tb), F32),
                        pltpu.VMEM((_ROUTE_HEADS, 2, PEER_TOPK, tb), I32),
                        pltpu.VMEM((PEER_SEL, tb), I32), pltpu.VMEM((PEER_SEL, tb), I32),
                        pltpu.VMEM((PEER_SEL, tb), F32)],
        compiler_params=_cparams(2), name="peer_route",
    )(q, lw["keys_hi"], lw["keys_lo"], *tables)


_CHUNK_ROWS = 32
_CHUNK = _CHUNK_ROWS * PEER_KEYS


def _up_body(h_ref, u_ref, i_ref, j_ref, g_ref, o_ref, val_ref):
    c = pl.program_id(1)

    @pl.when(c == 0)
    def _():
        val_ref[...] = jnp.zeros(val_ref.shape, F32)

    a = _dot_nt(h_ref[...], u_ref[...])
    iv, jv = i_ref[...], j_ref[...]
    acc = val_ref[...]
    for r in range(_CHUNK_ROWS):
        got = jnp.take_along_axis(a[:, r * PEER_KEYS:(r + 1) * PEER_KEYS], jv, axis=1)
        acc = acc + jnp.where(iv == c * _CHUNK_ROWS + r, got, 0.0)
    val_ref[...] = acc

    @pl.when(c == pl.num_programs(1) - 1)
    def _():
        o_ref[...] = g_ref[...] * (0.5 * acc * (1.0 + lax.erf(acc * (2.0 ** -0.5))))


def _up_call(h2, i_idx, j_idx, gate, lw):
    t = h2.shape[0]
    tb = min(t, 512)
    sel = pl.BlockSpec((tb, PEER_SEL), lambda i, c: (i, 0))
    return pl.pallas_call(
        _up_body, grid=(t // tb, N_EXPERTS // _CHUNK),
        in_specs=[pl.BlockSpec((tb, D_MODEL), lambda i, c: (i, 0)),
                  pl.BlockSpec((_CHUNK, D_MODEL), lambda i, c: (c, 0)), sel, sel, sel],
        out_specs=sel,
        out_shape=jax.ShapeDtypeStruct((t, PEER_SEL), F32),
        scratch_shapes=[pltpu.VMEM((tb, PEER_SEL), F32)],
        compiler_params=_cparams(2), name="peer_up",
    )(h2, lw["u"], i_idx, j_idx, gate)


_Y_PITCH = PEER_KEYS + SUBLANES


def _down_body(final, tb, x_ref, w_ref, i_ref, j_ref, v_ref, gf_ref, o_ref, y_scr, acc_ref):
    c = pl.program_id(1)

    @pl.when(c == 0)
    def _():
        acc_ref[...] = jnp.zeros(acc_ref.shape, F32)
        shape = (PEER_KEYS, PEER_SEL)
        sub = lax.broadcasted_iota(I32, shape, 0)

        def token(t, carry):
            wrow = jnp.broadcast_to(w_ref[pl.ds(t, 1), :], shape)
            irow = jnp.broadcast_to(i_ref[pl.ds(t, 1), :], shape)
            jrow = jnp.broadcast_to(j_ref[pl.ds(t, 1), :], shape)
            lhs = jnp.where(irow == sub, wrow, 0.0).astype(BF16)
            rhs = jnp.where(jrow == sub, 1.0, 0.0).astype(BF16)
            y_scr[pl.ds(pl.multiple_of(t * _Y_PITCH, SUBLANES), PEER_KEYS), :] = _dot_nt(lhs, rhs)
            return carry

        lax.fori_loop(0, tb, token, 0, unroll=8)

    base = c * _CHUNK_ROWS
    lhs = jnp.concatenate([y_scr[pl.ds(base + r, tb, stride=_Y_PITCH), :] for r in range(_CHUNK_ROWS)],
                          axis=1).astype(BF16)
    acc_ref[...] += _dot(lhs, v_ref[...])

    @pl.when(c == pl.num_programs(1) - 1)
    def _():
        xn = x_ref[...] + acc_ref[...]
        if final:
            xn = _rms(xn, gf_ref[...])
        o_ref[...] = xn


def _down_call(x1, wgt, i_idx, j_idx, lw, gfin, final):
    t = x1.shape[0]
    tb = min(t, 256)
    sel = pl.BlockSpec((tb, PEER_SEL), lambda i, c: (i, 0))
    tok = pl.BlockSpec((tb, D_MODEL), lambda i, c: (i, 0))
    return pl.pallas_call(
        functools.partial(_down_body, final, tb), grid=(t // tb, N_EXPERTS // _CHUNK),
        in_specs=[tok, sel, sel, sel, pl.BlockSpec((_CHUNK, D_MODEL), lambda i, c: (c, 0)),
                  pl.BlockSpec((1, D_MODEL), lambda i, c: (0, 0))],
        out_specs=tok,
        out_shape=jax.ShapeDtypeStruct((t, D_MODEL), F32),
        scratch_shapes=[pltpu.VMEM((tb * _Y_PITCH, PEER_KEYS), F32), pltpu.VMEM((tb, D_MODEL), F32)],
        compiler_params=_cparams(2), name="peer_down",
    )(x1, wgt, i_idx, j_idx, lw["v"], gfin)


def _split_bf16(w):
    hi = w.astype(BF16)
    return hi, (w - hi.astype(F32)).astype(BF16)


def _rot_half(w):
    half = MLA_ROPE // 2
    return jnp.concatenate([-w[..., half:], w[..., :half]], axis=-1)


def _prep_layer(l, p):
    w_in = p["w_in"][l]
    kpe_w = w_in[:, IN_W - MLA_ROPE:]
    w_in_ext = jnp.concatenate(
        [w_in[:, :IN_W - MLA_ROPE], jnp.zeros((D_MODEL, MLA_NOPE), F32), kpe_w, _rot_half(kpe_w)], axis=1)
    w_uq = p["mla_w_uq"][l]
    pad = jnp.zeros((MLA_Q, MLA_HEADS, MLA_HEAD_PAD - MLA_NOPE - MLA_ROPE), F32)
    wq = jnp.concatenate([w_uq, pad], axis=-1).reshape(MLA_Q, -1)
    wqr = jnp.concatenate([jnp.zeros((MLA_Q, MLA_HEADS, MLA_NOPE), F32), _rot_half(w_uq[..., MLA_NOPE:]), pad],
                          axis=-1).reshape(MLA_Q, -1)
    w_uk = p["mla_w_uk"][l]
    wk = jnp.concatenate([w_uk, jnp.zeros((MLA_KV, MLA_HEADS, MLA_HEAD_PAD - MLA_NOPE), F32)],
                         axis=-1).reshape(MLA_KV, -1)
    rows = jnp.arange(MLA_HEADS * MLA_HEAD_PAD)
    sel = ((rows[:, None] % MLA_HEAD_PAD) == (MLA_NOPE + jnp.arange(MLA_ROPE))[None, :]).astype(BF16)
    lp = p["diff_lambda"][l].astype(F32)
    lam_init = 0.8 - 0.6 * math.exp(-0.3 * l)
    lam = jnp.exp(jnp.sum(lp[0] * lp[1])) - jnp.exp(jnp.sum(lp[2] * lp[3])) + lam_init
    wq_hi, wq_lo = _split_bf16(p["peer_w_q"][l])
    keys_hi, keys_lo = _split_bf16(p["peer_sub_keys"][l])
    return {
        "g1": p["ln1_g"][l].reshape(1, -1), "g2": p["ln2_g"][l].reshape(1, -1),
        "w_in": w_in_ext.astype(BF16), "conv_w": p["conv_w"][l],
        "qg": p["mla_q_norm_g"][l].reshape(1, -1), "kvg": p["mla_kv_norm_g"][l].reshape(1, -1),
        "wq": wq.astype(BF16), "wqr": wqr.astype(BF16), "wk": wk.astype(BF16),
        "wv": p["mla_w_uv"][l].reshape(MLA_KV, MLA_W).astype(BF16), "sel": sel,
        "lam": lam.reshape(1), "lam11": lam.reshape(1, 1),
        "diff_gain": (jnp.tile(p["diff_norm_g"][l], DIFF_HEADS) * (1.0 - lam_init)).reshape(1, -1),
        "w_out": p["w_out"][l].astype(BF16), "wq_hi": wq_hi, "wq_lo": wq_lo,
        "keys_hi": keys_hi, "keys_lo": keys_lo,
        "u": p["peer_u"][l].astype(BF16), "v": p["peer_v"][l].astype(BF16),
    }


def _rope_tabs(pos):
    half = MLA_ROPE // 2
    inv = ROPE_THETA ** (-jnp.arange(half, dtype=F32) * (2.0 / MLA_ROPE))
    ang = pos.astype(F32)[:, None] * inv[None, :]
    cos2 = jnp.concatenate([jnp.cos(ang)] * 2, axis=-1)
    sin2 = jnp.concatenate([jnp.sin(ang)] * 2, axis=-1)
    n = pos.shape[0]
    z64 = jnp.zeros((n, MLA_NOPE), F32)
    z32 = jnp.zeros((n, LANES - MLA_NOPE - MLA_ROPE), F32)
    cq = jnp.concatenate([jnp.ones((n, MLA_NOPE), F32), cos2, z32], axis=1)
    ck = jnp.concatenate([z64, cos2, z32], axis=1)
    sn = jnp.concatenate([z64, sin2, z32], axis=1)
    return cq, ck, sn


def _peer_and_residual(x1, h2, q, lw, gfin, final):
    i_idx, j_idx, gate = _route_call(q, lw)
    wgt = _up_call(h2, i_idx, j_idx, gate, lw)
    return _down_call(x1, wgt, i_idx, j_idx, lw, gfin, final)


def _layer_prompt(x, lw, tabs, gfin, final, tq_diff=256, tq_mla=512, tk=512):
    b, s, _ = x.shape
    tabs3 = [t[None] for t in tabs]
    (y, dqb, dk, dkb, dv, dvb, ckv, kpe, qm, km, vm, newconv) = _proj_call(x, lw, tabs3)
    tk = min(tk, s)
    d_out = _diff_attn_call(dqb, dkb, dvb, lw["lam"], lw["diff_gain"], min(tq_diff, s), tk)
    m_out = _mla_attn_call(qm, km, vm, min(tq_mla, s), tk)
    t = b * s
    x1, h2, q = _post_call(x.reshape(t, -1), y.reshape(t, -1), d_out.reshape(t, -1), m_out.reshape(t, -1), lw)
    xn = _peer_and_residual(x1, h2, q, lw, gfin, final)
    return xn.reshape(b, s, -1), (newconv, dk, dv, ckv, kpe)


def _layer_sample(x, lw, tabs, gfin, final, layer, state, caches, page_table, sps=2, pps=16):
    ns = x.shape[0]
    x3 = x.reshape(1, ns, -1)
    tabs3 = [t[None] for t in tabs]
    st = (state[:, 0, :][None], state[:, 1, :][None])
    (y, dqb, dk, dkb, dv, dvb, ckv, kpe, qm, km, vm, z) = _proj_call(x3, lw, tabs3, st)
    del dkb, dvb, km, vm

    def per_tok(a):
        return a.reshape(ns, 1, -1)

    d_out, m_out = _decode_call(layer, page_table, caches, per_tok(dqb), per_tok(qm), per_tok(dk), per_tok(dv),
                                per_tok(ckv), per_tok(kpe), lw, sps, min(pps, page_table.shape[1]))
    x1, h2, q = _post_call(x.reshape(ns, -1), y.reshape(ns, -1), d_out.reshape(ns, -1), m_out.reshape(ns, -1), lw)
    xn = _peer_and_residual(x1, h2, q, lw, gfin, final)
    newconv = jnp.stack([state[:, 1, :], z.reshape(ns, -1)], axis=1)
    return xn.reshape(ns, 1, -1), (newconv, per_tok(dk), per_tok(dv), per_tok(ckv), per_tok(kpe))


def kernel(x_prompt, x_sample, state_conv, cache_diff_k, cache_diff_v, cache_mla_ckv, cache_mla_kpe,
           page_table, ln1_g, ln2_g, w_in, conv_w, diff_lambda, diff_norm_g, mla_q_norm_g, mla_kv_norm_g,
           mla_w_uq, mla_w_uk, mla_w_uv, w_out, peer_w_q, peer_sub_keys, peer_u, peer_v, final_norm_g):
    params = dict(ln1_g=ln1_g, ln2_g=ln2_g, w_in=w_in, conv_w=conv_w, diff_lambda=diff_lambda,
                  diff_norm_g=diff_norm_g, mla_q_norm_g=mla_q_norm_g, mla_kv_norm_g=mla_kv_norm_g,
                  mla_w_uq=mla_w_uq, mla_w_uk=mla_w_uk, mla_w_uv=mla_w_uv, w_out=w_out,
                  peer_w_q=peer_w_q, peer_sub_keys=peer_sub_keys, peer_u=peer_u, peer_v=peer_v)
    depth = ln1_g.shape[0]
    s = x_prompt.shape[1]
    ns = x_sample.shape[0]
    past_len = page_table.shape[1] * cache_diff_k.shape[2]
    tabs_p = _rope_tabs(jnp.arange(s))
    tabs_s = _rope_tabs(jnp.full((ns,), past_len, I32))
    gfin = final_norm_g.reshape(1, -1)
    caches = (cache_diff_k, cache_diff_v, cache_mla_ckv, cache_mla_kpe)

    xp, xs = x_prompt, x_sample.reshape(ns, -1)
    rows_p, rows_s = [], []
    for l in range(depth):
        lw = _prep_layer(l, params)
        final = l == depth - 1
        xp, new_p = _layer_prompt(xp, lw, tabs_p, gfin, final)
        xs3, new_s = _layer_sample(xs, lw, tabs_s, gfin, final, l, state_conv[l], caches, page_table)
        xs = xs3.reshape(ns, -1)
        rows_p.append(new_p)
        rows_s.append(new_s)
    outs_p = [jnp.stack(r) for r in zip(*rows_p)]
    outs_s = [jnp.stack(r) for r in zip(*rows_s)]
    return (xp, xs.reshape(ns, 1, -1), *outs_p, *outs_s)
```

```python
import functools
import math

import jax
import jax.numpy as jnp
from jax import lax
from jax.experimental import pallas as pl
from jax.experimental.pallas import tpu as pltpu

F32 = jnp.float32
BF16 = jnp.bfloat16
I32 = jnp.int32

D_MODEL = 1024
CONV_W = 256
CONV_K = 3
DIFF_HEADS = 4
DIFF_D = 32
DIFF_V = 64
DIFF_QK = 256
DIFF_W = 256
MLA_HEADS = 8
MLA_NOPE = 64
MLA_ROPE = 32
MLA_V = 64
MLA_KV = 256
MLA_Q = 384
MLA_W = 512
MLA_HEAD_PAD = 128
PEER_HEADS = 8
PEER_KEYS = 128
PEER_DKEY = 256
PEER_TOPK = 16
N_EXPERTS = PEER_KEYS * PEER_KEYS
PEER_SEL = PEER_HEADS * PEER_TOPK
ROPE_THETA = 10000.0
NORM_EPS = 1e-6
NEG_INF = -1e30
PAGE = 128
IN_W = 2208
IN_W_EXT = 2304
LANES = 128
SUBLANES = 8
VMEM_LIMIT = 56 * 1024 * 1024

DIFF_SCALE = DIFF_D ** -0.5
MLA_SCALE = (MLA_NOPE + MLA_ROPE) ** -0.5

_NT = (((1,), (1,)), ((), ()))


def _cparams(n_axes):
    return pltpu.CompilerParams(dimension_semantics=("arbitrary",) * n_axes,
                                vmem_limit_bytes=VMEM_LIMIT)


def _rms(x, g):
    ms = jnp.mean(x * x, axis=-1, keepdims=True)
    return x * lax.rsqrt(ms + NORM_EPS) * g


def _dot(a, b):
    return jnp.dot(a, b, preferred_element_type=F32)


def _dot_nt(a, b):
    return lax.dot_general(a, b, _NT, preferred_element_type=F32)


def _full_spec(shape):
    nd = len(shape)
    return pl.BlockSpec(shape, lambda *_: (0,) * nd)


def _proj_body(decode, tb, *refs):
    (x_ref, g1_ref, win_ref, cw_ref, qg_ref, kvg_ref, wq_ref, wqr_ref, wk_ref, wv_ref,
     cq_ref, ck_ref, sn_ref) = refs[:13]
    rest = refs[13:]
    if decode:
        s0_ref, s1_ref = rest[:2]
        rest = rest[2:]
    (y_ref, dqb_ref, dk_ref, dkb_ref, dv_ref, dvb_ref, ckv_ref, kpe_ref,
     qm_ref, km_ref, vm_ref, zo_ref) = rest[:12]

    x = x_ref[0]
    h = _rms(x, g1_ref[...]).astype(BF16)

    def proj(a, b):
        return _dot(h, win_ref[:, a:b])

    bg = proj(0, 256)
    z = proj(256, 512) * proj(512, 768)
    w0, w1, w2 = cw_ref[0:1, :], cw_ref[1:2, :], cw_ref[2:3, :]
    if decode:
        y = bg * (w0 * s0_ref[0] + w1 * s1_ref[0] + w2 * z)
        zo_ref[0] = z
    else:
        zs_ref = rest[12]
        j = pl.program_id(1)

        @pl.when(j == 0)
        def _():
            zs_ref[0:SUBLANES, :] = jnp.zeros((SUBLANES, CONV_W), F32)

        zs_ref[SUBLANES:SUBLANES + tb, :] = z
        zm1 = zs_ref[SUBLANES - 1:SUBLANES - 1 + tb, :]
        zm2 = zs_ref[SUBLANES - 2:SUBLANES - 2 + tb, :]
        y = bg * (w0 * zm2 + w1 * zm1 + w2 * z)
        zs_ref[0:SUBLANES, :] = z[tb - SUBLANES:tb, :]
        zo_ref[0] = z[tb - 2:tb, :]
    y_ref[0] = y.astype(BF16)

    dqb_ref[0] = (proj(768, 1024) * DIFF_SCALE).astype(BF16)
    dk = proj(1024, 1280)
    dk_ref[0] = dk
    dkb_ref[0] = dk.astype(BF16)
    dv = proj(1280, 1536)
    dv_ref[0] = dv
    dvb_ref[0] = dv.astype(BF16)

    cqn = _rms(proj(1536, 1920), qg_ref[...]).astype(BF16)
    cq8 = jnp.concatenate([cq_ref[0]] * MLA_HEADS, axis=1)
    sn8 = jnp.concatenate([sn_ref[0]] * MLA_HEADS, axis=1)
    qm = (_dot(cqn, wq_ref[...]) * cq8 + _dot(cqn, wqr_ref[...]) * sn8) * MLA_SCALE
    qm_ref[0] = qm.astype(BF16)

    ckvn = _rms(proj(1920, 2176), kvg_ref[...])
    ckv_ref[0] = ckvn
    ckb = ckvn.astype(BF16)
    tl = proj(2176, 2304)
    kr = tl * ck_ref[0] + pltpu.roll(tl, LANES - MLA_ROPE, 1) * sn_ref[0]
    kpe_ref[0] = kr[:, MLA_NOPE:MLA_NOPE + MLA_ROPE]
    km = _dot(ckb, wk_ref[...]) + jnp.concatenate([kr] * MLA_HEADS, axis=1)
    km_ref[0] = km.astype(BF16)
    vm_ref[0] = _dot(ckb, wv_ref[...]).astype(BF16)


def _proj_call(x3, lw, tabs, state=None):
    b, s, _ = x3.shape
    decode = state is not None
    tb = min(s, 512)
    assert s % tb == 0
    grid = (b, s // tb)

    def tok(w):
        return pl.BlockSpec((1, tb, w), lambda i, j: (i, j, 0))

    weights = [lw["g1"], lw["w_in"], lw["conv_w"], lw["qg"], lw["kvg"], lw["wq"], lw["wqr"], lw["wk"], lw["wv"]]
    tab = pl.BlockSpec((1, tb, LANES), lambda i, j: (0, j, 0))
    in_specs = [tok(D_MODEL)] + [_full_spec(w.shape) for w in weights] + [tab] * 3
    args = [x3] + weights + list(tabs)
    if decode:
        in_specs += [tok(CONV_W)] * 2
        args += list(state)

    def sds(w, dt):
        return jax.ShapeDtypeStruct((b, s, w), dt)

    out_shape = [sds(CONV_W, BF16), sds(DIFF_QK, BF16), sds(DIFF_QK, F32), sds(DIFF_QK, BF16),
                 sds(DIFF_W, F32), sds(DIFF_W, BF16), sds(MLA_KV, F32), sds(MLA_ROPE, F32),
                 sds(MLA_HEADS * MLA_HEAD_PAD, BF16), sds(MLA_HEADS * MLA_HEAD_PAD, BF16), sds(MLA_W, BF16)]
    out_specs = [tok(CONV_W), tok(DIFF_QK), tok(DIFF_QK), tok(DIFF_QK), tok(DIFF_W), tok(DIFF_W),
                 tok(MLA_KV), tok(MLA_ROPE), tok(MLA_HEADS * MLA_HEAD_PAD), tok(MLA_HEADS * MLA_HEAD_PAD),
                 tok(MLA_W)]
    scratch = []
    if decode:
        out_shape.append(sds(CONV_W, F32))
        out_specs.append(tok(CONV_W))
    else:
        out_shape.append(jax.ShapeDtypeStruct((b, CONV_K - 1, CONV_W), F32))
        out_specs.append(pl.BlockSpec((1, CONV_K - 1, CONV_W), lambda i, j: (i, 0, 0)))
        scratch.append(pltpu.VMEM((tb + SUBLANES, CONV_W), F32))
    return pl.pallas_call(
        functools.partial(_proj_body, decode, tb),
        grid=grid, in_specs=in_specs, out_specs=out_specs, out_shape=out_shape,
        scratch_shapes=scratch, compiler_params=_cparams(2), name="proj",
    )(*args)


def _flash_update(s, v, mask, m_ref, l_ref, acc_ref):
    if mask is not None:
        s = jnp.where(mask, s, NEG_INF)
    tiles = [s[:, c * LANES:(c + 1) * LANES] for c in range(s.shape[1] // LANES)]
    m_prev = m_ref[...]
    row_max = jnp.max(functools.reduce(jnp.maximum, tiles), axis=-1, keepdims=True)
    m_new = jnp.maximum(m_prev, row_max)
    alpha = jnp.exp(m_prev - m_new)
    p_tiles = [jnp.exp(t - m_new) for t in tiles]
    l_ref[...] = alpha * l_ref[...] + functools.reduce(jnp.add, p_tiles)
    p = jnp.concatenate(p_tiles, axis=1).astype(BF16)
    acc_ref[...] = alpha * acc_ref[...] + _dot(p, v)
    m_ref[...] = m_new


def _init_state(m_ref, l_ref, acc_ref):
    m_ref[...] = jnp.full(m_ref.shape, NEG_INF, F32)
    l_ref[...] = jnp.zeros(l_ref.shape, F32)
    acc_ref[...] = jnp.zeros(acc_ref.shape, F32)


def _causal_mask(rows, tq, tk, i, j):
    row = lax.broadcasted_iota(I32, (rows, tk), 0) & (tq - 1)
    col = lax.broadcasted_iota(I32, (rows, tk), 1)
    return col + j * tk <= row + i * tq


def _causal_sweep(tq, tk, i, scores, update, s_a, s_b):
    assert tk % tq == 0 and tq & (tq - 1) == 0
    j_diag = i // (tk // tq)
    n_pairs = j_diag // 2
    scores(0, s_a)

    def body(jj, c):
        j = 2 * jj
        scores(j + 1, s_b)
        update(s_a, j, False)
        scores(j + 2, s_a)
        update(s_b, j + 1, False)
        return c

    lax.fori_loop(0, n_pairs, body, 0)
    j0 = 2 * n_pairs

    @pl.when(j_diag == j0)
    def _():
        update(s_a, j0, True)

    @pl.when(j_diag != j0)
    def _():
        scores(j0 + 1, s_b)
        update(s_a, j0, False)
        update(s_b, j0 + 1, True)


def _diff_attn_body(tq, tk, lam_ref, q_ref, k_ref, v_ref, g_ref, o_ref, m_ref, l_ref, acc_ref, s_a, s_b):
    i = pl.program_id(2)
    q = q_ref[0].astype(F32)
    lane = lax.broadcasted_iota(I32, (1, LANES), 1)
    seg = lane >> 5
    qs = jnp.concatenate([jnp.where(seg == g, q, 0.0) for g in range(4)], axis=0).astype(BF16)
    _init_state(m_ref, l_ref, acc_ref)

    def scores(j, s_ref):
        s_ref[...] = _dot_nt(qs, k_ref[0, pl.ds(pl.multiple_of(j * tk, tk), tk), :])

    def update(s_ref, j, masked):
        v = v_ref[0, pl.ds(pl.multiple_of(j * tk, tk), tk), :]
        mask = _causal_mask(4 * tq, tq, tk, i, j) if masked else None
        _flash_update(s_ref[...], v, mask, m_ref, l_ref, acc_ref)

    _causal_sweep(tq, tk, i, scores, update, s_a, s_b)

    lam = lam_ref[0]
    o_all = acc_ref[...] / jnp.sum(l_ref[...], axis=-1, keepdims=True)
    o0 = o_all[0:tq] - lam * o_all[tq:2 * tq]
    o1 = o_all[2 * tq:3 * tq] - lam * o_all[3 * tq:4 * tq]
    first = lane < DIFF_V
    o = jnp.where(first, o0, o1)
    sq = o * o
    ss0 = jnp.sum(jnp.where(first, sq, 0.0), axis=-1, keepdims=True)
    ss1 = jnp.sum(jnp.where(first, 0.0, sq), axis=-1, keepdims=True)
    ms = jnp.where(first, ss0, ss1) * (1.0 / DIFF_V)
    o_ref[0] = (o * lax.rsqrt(ms + NORM_EPS) * g_ref[...]).astype(BF16)


def _diff_attn_call(dq, dk, dv, lam, gain, tq, tk):
    b, s, _ = dq.shape
    n_pairs = DIFF_QK // LANES
    grid = (b, n_pairs, s // tq)
    return pl.pallas_call(
        functools.partial(_diff_attn_body, tq, tk),
        grid=grid,
        in_specs=[pl.BlockSpec(memory_space=pltpu.SMEM),
                  pl.BlockSpec((1, tq, LANES), lambda bi, hp, i: (bi, i, hp)),
                  pl.BlockSpec((1, s, LANES), lambda bi, hp, i: (bi, 0, hp)),
                  pl.BlockSpec((1, s, LANES), lambda bi, hp, i: (bi, 0, hp)),
                  pl.BlockSpec((1, LANES), lambda bi, hp, i: (0, hp))],
        out_specs=pl.BlockSpec((1, tq, LANES), lambda bi, hp, i: (bi, i, hp)),
        out_shape=jax.ShapeDtypeStruct((b, s, DIFF_W), BF16),
        scratch_shapes=[pltpu.VMEM((4 * tq, LANES), F32)] * 3 + [pltpu.VMEM((4 * tq, tk), F32)] * 2,
        compiler_params=_cparams(3), name="diff_attn",
    )(lam, dq, dk, dv, gain)


def _mla_attn_body(tq, tk, q_ref, k_ref, v_ref, o_ref, m_ref, l_ref, acc_ref, s_a, s_b):
    i = pl.program_id(2)
    q = q_ref[0]
    _init_state(m_ref, l_ref, acc_ref)

    def scores(j, s_ref):
        k = k_ref[0, pl.ds(pl.multiple_of(j * tk, tk), tk), :]
        for hh in range(2):
            sl = slice(hh * MLA_HEAD_PAD, (hh + 1) * MLA_HEAD_PAD)
            s_ref[hh] = _dot_nt(q[:, sl], k[:, sl])

    def update(s_ref, j, masked):
        v = v_ref[0, pl.ds(pl.multiple_of(j * tk, tk), tk), :]
        mask = _causal_mask(tq, tq, tk, i, j) if masked else None
        for hh in range(2):
            _flash_update(s_ref[hh], v, mask, m_ref.at[hh], l_ref.at[hh], acc_ref.at[hh])

    _causal_sweep(tq, tk, i, scores, update, s_a, s_b)
    lane = lax.broadcasted_iota(I32, (1, LANES), 1)
    o = acc_ref[...] / jnp.sum(l_ref[...], axis=-1, keepdims=True)
    o_ref[0] = jnp.where(lane < MLA_V, o[0], o[1]).astype(BF16)


def _mla_attn_call(qm, km, vm, tq, tk):
    b, s, _ = qm.shape
    n_pairs = MLA_HEADS // 2
    grid = (b, n_pairs, s // tq)
    qw = 2 * MLA_HEAD_PAD
    return pl.pallas_call(
        functools.partial(_mla_attn_body, tq, tk),
        grid=grid,
        in_specs=[pl.BlockSpec((1, tq, qw), lambda bi, hp, i: (bi, i, hp)),
                  pl.BlockSpec((1, s, qw), lambda bi, hp, i: (bi, 0, hp)),
                  pl.BlockSpec((1, s, LANES), lambda bi, hp, i: (bi, 0, hp))],
        out_specs=pl.BlockSpec((1, tq, LANES), lambda bi, hp, i: (bi, i, hp)),
        out_shape=jax.ShapeDtypeStruct((b, s, MLA_W), BF16),
        scratch_shapes=[pltpu.VMEM((2, tq, LANES), F32)] * 3 + [pltpu.VMEM((2, tq, tk), F32)] * 2,
        compiler_params=_cparams(3), name="mla_attn",
    )(qm, km, vm)


def _decode_body(layer, sps, pps, n_pages, *refs):
    pt_ref = refs[0]
    (dq_ref, qm_ref, dks_ref, dvs_ref, ckvs_ref, kpes_ref, wk_ref, sel_ref, wuv_ref,
     gain_ref, lam_ref) = refs[1:12]
    cache_refs = refs[12:16]
    do_ref, mo_ref = refs[16:18]
    (qbd_ref, qlat_ref, qpe_ref, md_ref, ld_ref, accd_ref, mm_ref, lm_ref, accm_ref) = refs[18:27]
    bufs = refs[27:31]
    sem = refs[31]

    bi = pl.program_id(0)
    p = pl.program_id(1)
    n_steps = pl.num_programs(1)
    g = bi * n_steps + p
    n_rows = SUBLANES

    def page_copies(step_bi, step_p, slot, lookup):
        out = []
        for u in range(sps):
            for r in range(pps):
                page = pt_ref[(step_bi * sps + u) * n_pages + step_p * pps + r] if lookup else 0
                for c in range(4):
                    out.append(pltpu.make_async_copy(cache_refs[c].at[layer, page], bufs[c].at[slot, u * pps + r],
                                                     sem.at[slot, c]))
        return out

    @pl.when(g == 0)
    def _():
        for cp in page_copies(bi, p, 0, True):
            cp.start()

    @pl.when(g + 1 < pl.num_programs(0) * n_steps)
    def _():
        wrap = p == n_steps - 1
        for cp in page_copies(jnp.where(wrap, bi + 1, bi), jnp.where(wrap, 0, p + 1), (g + 1) & 1, True):
            cp.start()

    slot = g & 1
    for cp in page_copies(bi, p, slot, False):
        cp.wait()

    @pl.when(p == 0)
    def _():
        for u in range(sps):
            sub = lax.broadcasted_iota(I32, (n_rows, DIFF_QK), 0)
            lane = lax.broadcasted_iota(I32, (n_rows, DIFF_QK), 1)
            row = jnp.broadcast_to(dq_ref[u].astype(F32), (n_rows, DIFF_QK))
            qbd = jnp.where((lane >> 5) == sub, row, 0.0)
            qbd_ref[u] = qbd.astype(BF16)
            md_ref[u] = jnp.sum(qbd * dks_ref[u], axis=1, keepdims=True)
            ld_ref[u] = jnp.ones((n_rows, 1), F32)
            accd_ref[u] = jnp.broadcast_to(dvs_ref[u], (n_rows, DIFF_W))

            wq = MLA_HEADS * MLA_HEAD_PAD
            sub = lax.broadcasted_iota(I32, (n_rows, wq), 0)
            lane = lax.broadcasted_iota(I32, (n_rows, wq), 1)
            row = jnp.broadcast_to(qm_ref[u].astype(F32), (n_rows, wq))
            qf = jnp.where((lane >> 7) == sub, row, 0.0).astype(BF16)
            qlat = _dot_nt(qf, wk_ref[...]).astype(BF16)
            qpe = _dot(qf, sel_ref[...]).astype(BF16)
            qlat_ref[u] = qlat
            qpe_ref[u] = qpe
            mm_ref[u] = (jnp.sum(qlat.astype(F32) * ckvs_ref[u], axis=1, keepdims=True)
                         + jnp.sum(qpe.astype(F32) * kpes_ref[u], axis=1, keepdims=True))
            lm_ref[u] = jnp.ones((n_rows, 1), F32)
            accm_ref[u] = jnp.broadcast_to(ckvs_ref[u], (n_rows, MLA_KV))

    def rows(c, u):
        x = bufs[c][slot, u * pps:(u + 1) * pps]
        return x.reshape(pps * PAGE, x.shape[-1]).astype(BF16)

    def update(s, v, u, m_ref, l_ref, acc_ref):
        m_prev = m_ref[u]
        m_new = jnp.maximum(m_prev, jnp.max(s, axis=-1, keepdims=True))
        alpha = jnp.exp(m_prev - m_new)
        pr = jnp.exp(s - m_new)
        l_ref[u] = alpha * l_ref[u] + jnp.sum(pr, axis=-1, keepdims=True)
        acc_ref[u] = alpha * acc_ref[u] + _dot(pr.astype(BF16), v)
        m_ref[u] = m_new

    for u in range(sps):
        update(_dot_nt(qbd_ref[u], rows(0, u)), rows(1, u), u, md_ref, ld_ref, accd_ref)
        c = rows(2, u)
        kpe_t = jnp.concatenate([bufs[3][slot, u * pps + r] for r in range(pps)], axis=1).astype(BF16)
        update(_dot_nt(qlat_ref[u], c) + _dot(qpe_ref[u], kpe_t), c, u, mm_ref, lm_ref, accm_ref)

    @pl.when(p == n_steps - 1)
    def _():
        lam = lam_ref[...]
        for u in range(sps):
            sub = lax.broadcasted_iota(I32, (n_rows, DIFF_W), 0)
            lane = lax.broadcasted_iota(I32, (n_rows, DIFF_W), 1)
            coef = jnp.where((sub & 1) == 0, 1.0, -lam)
            o2 = accd_ref[u] / ld_ref[u] * coef
            orow = jnp.sum(jnp.where((lane >> 6) == (sub >> 1), o2, 0.0), axis=0, keepdims=True)
            lane1 = lane[0:1, :] >> 6
            sq = orow * orow
            ms = jnp.zeros_like(orow)
            for g in range(DIFF_HEADS):
                ssg = jnp.sum(jnp.where(lane1 == g, sq, 0.0), axis=-1, keepdims=True)
                ms = jnp.where(lane1 == g, ssg, ms)
            ms = ms * (1.0 / DIFF_V)
            do_ref[u] = (orow * lax.rsqrt(ms + NORM_EPS) * gain_ref[...]).astype(BF16)

            olat = (accm_ref[u] / lm_ref[u]).astype(BF16)
            r = _dot(olat, wuv_ref[...])
            sub = lax.broadcasted_iota(I32, (n_rows, MLA_W), 0)
            lane = lax.broadcasted_iota(I32, (n_rows, MLA_W), 1)
            mo_ref[u] = jnp.sum(jnp.where((lane >> 6) == sub, r, 0.0), axis=0, keepdims=True).astype(BF16)


def _decode_call(layer, page_table, caches, dq, qm, dks, dvs, ckvs, kpes, lw, sps, pps):
    ns, n_pages = page_table.shape
    assert n_pages % pps == 0 and ns % sps == 0
    n_steps = n_pages // pps
    cache_k, cache_v, cache_ckv, cache_kpe = caches
    cache_kpe_t = jnp.swapaxes(cache_kpe, 2, 3)

    def tok(w):
        return pl.BlockSpec((sps, 1, w), lambda bi, p, pt: (bi, 0, 0))

    def full(shape):
        nd = len(shape)
        return pl.BlockSpec(shape, lambda bi, p, pt: (0,) * nd)

    weights = [lw["wk"], lw["sel"], lw["wv"], lw["diff_gain"], lw["lam11"]]
    n_pg = sps * pps
    cache_args = [cache_k, cache_v, cache_ckv, cache_kpe_t]
    in_specs = ([tok(DIFF_QK), tok(MLA_HEADS * MLA_HEAD_PAD), tok(DIFF_QK), tok(DIFF_W), tok(MLA_KV), tok(MLA_ROPE)]
                + [full(w.shape) for w in weights]
                + [pl.BlockSpec(memory_space=pl.ANY)] * len(cache_args))
    args = [dq, qm, dks, dvs, ckvs, kpes] + weights + cache_args

    def state(w, dt=F32):
        return pltpu.VMEM((sps, SUBLANES, w), dt)

    n_slots = 2
    page_bufs = [pltpu.VMEM((n_slots, n_pg) + c.shape[2:], c.dtype) for c in cache_args]
    grid_spec = pltpu.PrefetchScalarGridSpec(
        num_scalar_prefetch=1, grid=(ns // sps, n_steps), in_specs=in_specs,
        out_specs=[tok(DIFF_W), tok(MLA_W)],
        scratch_shapes=[state(DIFF_QK, BF16), state(MLA_KV, BF16), state(MLA_ROPE, BF16),
                        state(1), state(1), state(DIFF_W), state(1), state(1), state(MLA_KV)]
        + page_bufs + [pltpu.SemaphoreType.DMA((n_slots, len(cache_args)))])
    return pl.pallas_call(
        functools.partial(_decode_body, layer, sps, pps, n_pages),
        grid_spec=grid_spec,
        out_shape=[jax.ShapeDtypeStruct((ns, 1, DIFF_W), BF16), jax.ShapeDtypeStruct((ns, 1, MLA_W), BF16)],
        compiler_params=_cparams(2), name="decode_attn",
    )(page_table.reshape(-1), *args)


def _post_body(x_ref, y_ref, d_ref, m_ref, wo_ref, g2_ref, wqh_ref, wql_ref, x1_ref, h2_ref, q_ref):
    x1 = (x_ref[...] + _dot(y_ref[...], wo_ref[0:CONV_W, :])
          + _dot(d_ref[...], wo_ref[CONV_W:CONV_W + DIFF_W, :])
          + _dot(m_ref[...], wo_ref[CONV_W + DIFF_W:, :]))
    x1_ref[...] = x1
    h2 = _rms(x1, g2_ref[...])
    hb = h2.astype(BF16)
    hl = (h2 - hb.astype(F32)).astype(BF16)
    q_ref[...] = _dot(hb, wqh_ref[...]) + _dot(hb, wql_ref[...]) + _dot(hl, wqh_ref[...])
    h2_ref[...] = hb


def _post_call(x, y, d, m, lw):
    t = x.shape[0]
    tb = min(t, 256)
    qw = PEER_HEADS * PEER_DKEY

    def tok(w):
        return pl.BlockSpec((tb, w), lambda i: (i, 0))

    weights = [lw["w_out"], lw["g2"], lw["wq_hi"], lw["wq_lo"]]
    return pl.pallas_call(
        _post_body, grid=(t // tb,),
        in_specs=[tok(D_MODEL), tok(CONV_W), tok(DIFF_W), tok(MLA_W)] + [_full_spec(w.shape) for w in weights],
        out_specs=[tok(D_MODEL), tok(D_MODEL), tok(qw)],
        out_shape=[jax.ShapeDtypeStruct((t, D_MODEL), F32), jax.ShapeDtypeStruct((t, D_MODEL), BF16),
                   jax.ShapeDtypeStruct((t, qw), F32)],
        compiler_params=_cparams(1), name="post_attn",
    )(x, y, d, m, *weights)


_CAND_PAIRS = [(a, b_) for a in range(PEER_TOPK) for b_ in range(PEER_TOPK) if (a + 1) * (b_ + 1) <= PEER_TOPK]
_CAND_ROWS = -(-len(_CAND_PAIRS) // SUBLANES) * SUBLANES


def _cand_tables():
    n_pad = _CAND_ROWS - len(_CAND_PAIRS)
    flat = [a * PEER_TOPK + b_ for a, b_ in _CAND_PAIRS] + [PEER_TOPK * PEER_TOPK] * n_pad
    pen = [0.0] * len(_CAND_PAIRS) + [-math.inf] * n_pad
    rows = jnp.arange(PEER_TOPK)[None, :]
    pick_a = (jnp.asarray([a for a, _ in _CAND_PAIRS] + [0] * n_pad)[:, None] == rows).astype(BF16)
    pick_b = (jnp.asarray([b_ for _, b_ in _CAND_PAIRS] + [0] * n_pad)[:, None] == rows).astype(BF16)
    return (jnp.asarray(flat, I32).reshape(-1, 1), jnp.asarray(pen, F32).reshape(-1, 1), pick_a, pick_b)


def _pick_rows(onehot, x):
    hi = x.astype(BF16)
    r1 = x - hi.astype(F32)
    mid = r1.astype(BF16)
    lo = (r1 - mid.astype(F32)).astype(BF16)
    return (_dot(onehot, hi) + _dot(onehot, mid)) + _dot(onehot, lo)


_ROUTE_HEADS = 4


def _route_body(tb, q_ref, khi_ref, klo_ref, flat_ref, pen_ref, pa_ref, pb_ref, io_ref, jo_ref, go_ref,
                sv_ref, si_ref, i_scr, j_scr, g_scr):
    hg = pl.program_id(1)
    q = q_ref[...]
    qh = q.astype(BF16)
    ql = (q - qh.astype(F32)).astype(BF16)
    n_iota = lax.broadcasted_iota(I32, (PEER_KEYS, tb), 0)

    for u in range(_ROUTE_HEADS):
        def sub_scores(p, u=u):
            c0 = (2 * u + p) * PEER_KEYS
            kh, kl = khi_ref[u, p], klo_ref[u, p]
            qh_, ql_ = qh[:, c0:c0 + PEER_KEYS], ql[:, c0:c0 + PEER_KEYS]
            return _dot_nt(kh, qh_) + _dot_nt(kh, ql_) + _dot_nt(kl, qh_)

        def it(k, ss, u=u):
            out = []
            for p, s in enumerate(ss):
                m = jnp.max(s, axis=0, keepdims=True)
                idx = jnp.min(jnp.where(s == m, n_iota, PEER_KEYS), axis=0, keepdims=True)
                sv_ref[u, p, pl.ds(k, 1), :] = m
                si_ref[u, p, pl.ds(k, 1), :] = idx
                out.append(jnp.where(n_iota == idx, -jnp.inf, s))
            return tuple(out)

        lax.fori_loop(0, PEER_TOPK, it, (sub_scores(0), sub_scores(1)))

    cands = tuple((_pick_rows(pa_ref[...], sv_ref[u, 0]) + _pick_rows(pb_ref[...], sv_ref[u, 1])) + pen_ref[...]
                  for u in range(_ROUTE_HEADS))
    flat = jnp.broadcast_to(flat_ref[...], (_CAND_ROWS, tb))
    k_iota = lax.broadcasted_iota(I32, (PEER_TOPK, tb), 0)
    base = pl.multiple_of(hg * (_ROUTE_HEADS * PEER_TOPK), _ROUTE_HEADS * PEER_TOPK)

    def it2(k, cs):
        out = []
        for u, cand in enumerate(cs):
            row = base + u * PEER_TOPK + k
            m = jnp.max(cand, axis=0, keepdims=True)
            fl = jnp.min(jnp.where(cand == m, flat, PEER_TOPK * PEER_TOPK), axis=0, keepdims=True)
            a = fl >> 4
            b_ = fl & (PEER_TOPK - 1)
            g_scr[pl.ds(row, 1), :] = m
            i_scr[pl.ds(row, 1), :] = jnp.sum(jnp.where(k_iota == a, si_ref[u, 0], 0), axis=0, keepdims=True)
            j_scr[pl.ds(row, 1), :] = jnp.sum(jnp.where(k_iota == b_, si_ref[u, 1], 0), axis=0, keepdims=True)
            out.append(jnp.where(flat == fl, -jnp.inf, cand))
        return tuple(out)

    lax.fori_loop(0, PEER_TOPK, it2, cands)
    for u in range(_ROUTE_HEADS):
        rows = pl.ds(base + u * PEER_TOPK, PEER_TOPK)
        ts = g_scr[rows, :]
        e = jnp.exp(ts - ts[0:1, :])
        g_scr[rows, :] = e / jnp.sum(e, axis=0, keepdims=True)

    @pl.when(hg == pl.num_programs(1) - 1)
    def _():
        io_ref[...] = i_scr[...].T
        jo_ref[...] = j_scr[...].T
        go_ref[...] = g_scr[...].T


def _route_call(q, lw):
    t = q.shape[0]
    tb = min(t, 256)
    tables = _cand_tables()
    tokq = pl.BlockSpec((tb, _ROUTE_HEADS * PEER_DKEY), lambda i, h: (i, h))
    keys = pl.BlockSpec((_ROUTE_HEADS, 2, PEER_KEYS, PEER_DKEY // 2), lambda i, h: (h, 0, 0, 0))
    out = pl.BlockSpec((tb, PEER_SEL), lambda i, h: (i, 0))
    return pl.pallas_call(
        functools.partial(_route_body, tb), grid=(t // tb, PEER_HEADS // _ROUTE_HEADS),
        in_specs=[tokq, keys, keys] + [pl.BlockSpec(tb_.shape, lambda i, h: (0, 0)) for tb_ in tables],
        out_specs=[out, out, out],
        out_shape=[jax.ShapeDtypeStruct((t, PEER_SEL), I32), jax.ShapeDtypeStruct((t, PEER_SEL), I32),
                   jax.ShapeDtypeStruct((t, PEER_SEL), F32)],
        scratch_shapes=[pltpu.VMEM((_ROUTE_HEADS, 2, PEER_TOPK, tb), F32),
                        pltpu.VMEM((_ROUTE_HEADS, 2, PEER_TOPK, tb), I32),
                        pltpu.VMEM((PEER_SEL, tb), I32), pltpu.VMEM((PEER_SEL, tb), I32),
                        pltpu.VMEM((PEER_SEL, tb), F32)],
        compiler_params=_cparams(2), name="peer_route",
    )(q, lw["keys_hi"], lw["keys_lo"], *tables)


_CHUNK_ROWS = 32
_CHUNK = _CHUNK_ROWS * PEER_KEYS


def _up_body(h_ref, u_ref, i_ref, j_ref, g_ref, o_ref, val_ref):
    c = pl.program_id(1)

    @pl.when(c == 0)
    def _():
        val_ref[...] = jnp.zeros(val_ref.shape, F32)

    a = _dot_nt(h_ref[...], u_ref[...])
    iv, jv = i_ref[...], j_ref[...]
    acc = val_ref[...]
    for r in range(_CHUNK_ROWS):
        got = jnp.take_along_axis(a[:, r * PEER_KEYS:(r + 1) * PEER_KEYS], jv, axis=1)
        acc = acc + jnp.where(iv == c * _CHUNK_ROWS + r, got, 0.0)
    val_ref[...] = acc

    @pl.when(c == pl.num_programs(1) - 1)
    def _():
        o_ref[...] = g_ref[...] * (0.5 * acc * (1.0 + lax.erf(acc * (2.0 ** -0.5))))


def _up_call(h2, i_idx, j_idx, gate, lw):
    t = h2.shape[0]
    tb = min(t, 512)
    sel = pl.BlockSpec((tb, PEER_SEL), lambda i, c: (i, 0))
    return pl.pallas_call(
        _up_body, grid=(t // tb, N_EXPERTS // _CHUNK),
        in_specs=[pl.BlockSpec((tb, D_MODEL), lambda i, c: (i, 0)),
                  pl.BlockSpec((_CHUNK, D_MODEL), lambda i, c: (c, 0)), sel, sel, sel],
        out_specs=sel,
        out_shape=jax.ShapeDtypeStruct((t, PEER_SEL), F32),
        scratch_shapes=[pltpu.VMEM((tb, PEER_SEL), F32)],
        compiler_params=_cparams(2), name="peer_up",
    )(h2, lw["u"], i_idx, j_idx, gate)


_Y_PITCH = PEER_KEYS + SUBLANES


def _down_body(final, tb, x_ref, w_ref, i_ref, j_ref, v_ref, gf_ref, o_ref, y_scr, acc_ref):
    c = pl.program_id(1)

    @pl.when(c == 0)
    def _():
        acc_ref[...] = jnp.zeros(acc_ref.shape, F32)
        shape = (PEER_KEYS, PEER_SEL)
        sub = lax.broadcasted_iota(I32, shape, 0)

        def token(t, carry):
            wrow = jnp.broadcast_to(w_ref[pl.ds(t, 1), :], shape)
            irow = jnp.broadcast_to(i_ref[pl.ds(t, 1), :], shape)
            jrow = jnp.broadcast_to(j_ref[pl.ds(t, 1), :], shape)
            lhs = jnp.where(irow == sub, wrow, 0.0).astype(BF16)
            rhs = jnp.where(jrow == sub, 1.0, 0.0).astype(BF16)
            y_scr[pl.ds(pl.multiple_of(t * _Y_PITCH, SUBLANES), PEER_KEYS), :] = _dot_nt(lhs, rhs)
            return carry

        lax.fori_loop(0, tb, token, 0, unroll=8)

    base = c * _CHUNK_ROWS
    lhs = jnp.concatenate([y_scr[pl.ds(base + r, tb, stride=_Y_PITCH), :] for r in range(_CHUNK_ROWS)],
                          axis=1).astype(BF16)
    acc_ref[...] += _dot(lhs, v_ref[...])

    @pl.when(c == pl.num_programs(1) - 1)
    def _():
        xn = x_ref[...] + acc_ref[...]
        if final:
            xn = _rms(xn, gf_ref[...])
        o_ref[...] = xn


def _down_call(x1, wgt, i_idx, j_idx, lw, gfin, final):
    t = x1.shape[0]
    tb = min(t, 256)
    sel = pl.BlockSpec((tb, PEER_SEL), lambda i, c: (i, 0))
    tok = pl.BlockSpec((tb, D_MODEL), lambda i, c: (i, 0))
    return pl.pallas_call(
        functools.partial(_down_body, final, tb), grid=(t // tb, N_EXPERTS // _CHUNK),
        in_specs=[tok, sel, sel, sel, pl.BlockSpec((_CHUNK, D_MODEL), lambda i, c: (c, 0)),
                  pl.BlockSpec((1, D_MODEL), lambda i, c: (0, 0))],
        out_specs=tok,
        out_shape=jax.ShapeDtypeStruct((t, D_MODEL), F32),
        scratch_shapes=[pltpu.VMEM((tb * _Y_PITCH, PEER_KEYS), F32), pltpu.VMEM((tb, D_MODEL), F32)],
        compiler_params=_cparams(2), name="peer_down",
    )(x1, wgt, i_idx, j_idx, lw["v"], gfin)


def _split_bf16(w):
    hi = w.astype(BF16)
    return hi, (w - hi.astype(F32)).astype(BF16)


def _rot_half(w):
    half = MLA_ROPE // 2
    return jnp.concatenate([-w[..., half:], w[..., :half]], axis=-1)


def _prep_layer(l, p):
    w_in = p["w_in"][l]
    kpe_w = w_in[:, IN_W - MLA_ROPE:]
    w_in_ext = jnp.concatenate(
        [w_in[:, :IN_W - MLA_ROPE], jnp.zeros((D_MODEL, MLA_NOPE), F32), kpe_w, _rot_half(kpe_w)], axis=1)
    w_uq = p["mla_w_uq"][l]
    pad = jnp.zeros((MLA_Q, MLA_HEADS, MLA_HEAD_PAD - MLA_NOPE - MLA_ROPE), F32)
    wq = jnp.concatenate([w_uq, pad], axis=-1).reshape(MLA_Q, -1)
    wqr = jnp.concatenate([jnp.zeros((MLA_Q, MLA_HEADS, MLA_NOPE), F32), _rot_half(w_uq[..., MLA_NOPE:]), pad],
                          axis=-1).reshape(MLA_Q, -1)
    w_uk = p["mla_w_uk"][l]
    wk = jnp.concatenate([w_uk, jnp.zeros((MLA_KV, MLA_HEADS, MLA_HEAD_PAD - MLA_NOPE), F32)],
                         axis=-1).reshape(MLA_KV, -1)
    rows = jnp.arange(MLA_HEADS * MLA_HEAD_PAD)
    sel = ((rows[:, None] % MLA_HEAD_PAD) == (MLA_NOPE + jnp.arange(MLA_ROPE))[None, :]).astype(BF16)
    lp = p["diff_lambda"][l].astype(F32)
    lam_init = 0.8 - 0.6 * math.exp(-0.3 * l)
    lam = jnp.exp(jnp.sum(lp[0] * lp[1])) - jnp.exp(jnp.sum(lp[2] * lp[3])) + lam_init
    wq_hi, wq_lo = _split_bf16(p["peer_w_q"][l])
    keys_hi, keys_lo = _split_bf16(p["peer_sub_keys"][l])
    return {
        "g1": p["ln1_g"][l].reshape(1, -1), "g2": p["ln2_g"][l].reshape(1, -1),
        "w_in": w_in_ext.astype(BF16), "conv_w": p["conv_w"][l],
        "qg": p["mla_q_norm_g"][l].reshape(1, -1), "kvg": p["mla_kv_norm_g"][l].reshape(1, -1),
        "wq": wq.astype(BF16), "wqr": wqr.astype(BF16), "wk": wk.astype(BF16),
        "wv": p["mla_w_uv"][l].reshape(MLA_KV, MLA_W).astype(BF16), "sel": sel,
        "lam": lam.reshape(1), "lam11": lam.reshape(1, 1),
        "diff_gain": (jnp.tile(p["diff_norm_g"][l], DIFF_HEADS) * (1.0 - lam_init)).reshape(1, -1),
        "w_out": p["w_out"][l].astype(BF16), "wq_hi": wq_hi, "wq_lo": wq_lo,
        "keys_hi": keys_hi, "keys_lo": keys_lo,
        "u": p["peer_u"][l].astype(BF16), "v": p["peer_v"][l].astype(BF16),
    }


def _rope_tabs(pos):
    half = MLA_ROPE // 2
    inv = ROPE_THETA ** (-jnp.arange(half, dtype=F32) * (2.0 / MLA_ROPE))
    ang = pos.astype(F32)[:, None] * inv[None, :]
    cos2 = jnp.concatenate([jnp.cos(ang)] * 2, axis=-1)
    sin2 = jnp.concatenate([jnp.sin(ang)] * 2, axis=-1)
    n = pos.shape[0]
    z64 = jnp.zeros((n, MLA_NOPE), F32)
    z32 = jnp.zeros((n, LANES - MLA_NOPE - MLA_ROPE), F32)
    cq = jnp.concatenate([jnp.ones((n, MLA_NOPE), F32), cos2, z32], axis=1)
    ck = jnp.concatenate([z64, cos2, z32], axis=1)
    sn = jnp.concatenate([z64, sin2, z32], axis=1)
    return cq, ck, sn


def _peer_and_residual(x1, h2, q, lw, gfin, final):
    i_idx, j_idx, gate = _route_call(q, lw)
    wgt = _up_call(h2, i_idx, j_idx, gate, lw)
    return _down_call(x1, wgt, i_idx, j_idx, lw, gfin, final)


def _layer_prompt(x, lw, tabs, gfin, final, tq_diff=256, tq_mla=512, tk=512):
    b, s, _ = x.shape
    tabs3 = [t[None] for t in tabs]
    (y, dqb, dk, dkb, dv, dvb, ckv, kpe, qm, km, vm, newconv) = _proj_call(x, lw, tabs3)
    tk = min(tk, s)
    d_out = _diff_attn_call(dqb, dkb, dvb, lw["lam"], lw["diff_gain"], min(tq_diff, s), tk)
    m_out = _mla_attn_call(qm, km, vm, min(tq_mla, s), tk)
    t = b * s
    x1, h2, q = _post_call(x.reshape(t, -1), y.reshape(t, -1), d_out.reshape(t, -1), m_out.reshape(t, -1), lw)
    xn = _peer_and_residual(x1, h2, q, lw, gfin, final)
    return xn.reshape(b, s, -1), (newconv, dk, dv, ckv, kpe)


def _layer_sample(x, lw, tabs, gfin, final, layer, state, caches, page_table, sps=2, pps=16):
    ns = x.shape[0]
    x3 = x.reshape(1, ns, -1)
    tabs3 = [t[None] for t in tabs]
    st = (state[:, 0, :][None], state[:, 1, :][None])
    (y, dqb, dk, dkb, dv, dvb, ckv, kpe, qm, km, vm, z) = _proj_call(x3, lw, tabs3, st)
    del dkb, dvb, km, vm

    def per_tok(a):
        return a.reshape(ns, 1, -1)

    d_out, m_out = _decode_call(layer, page_table, caches, per_tok(dqb), per_tok(qm), per_tok(dk), per_tok(dv),
                                per_tok(ckv), per_tok(kpe), lw, sps, min(pps, page_table.shape[1]))
    x1, h2, q = _post_call(x.reshape(ns, -1), y.reshape(ns, -1), d_out.reshape(ns, -1), m_out.reshape(ns, -1), lw)
    xn = _peer_and_residual(x1, h2, q, lw, gfin, final)
    newconv = jnp.stack([state[:, 1, :], z.reshape(ns, -1)], axis=1)
    return xn.reshape(ns, 1, -1), (newconv, per_tok(dk), per_tok(dv), per_tok(ckv), per_tok(kpe))


def kernel(x_prompt, x_sample, state_conv, cache_diff_k, cache_diff_v, cache_mla_ckv, cache_mla_kpe,
           page_table, ln1_g, ln2_g, w_in, conv_w, diff_lambda, diff_norm_g, mla_q_norm_g, mla_kv_norm_g,
           mla_w_uq, mla_w_uk, mla_w_uv, w_out, peer_w_q, peer_sub_keys, peer_u, peer_v, final_norm_g):
    params = dict(ln1_g=ln1_g, ln2_g=ln2_g, w_in=w_in, conv_w=conv_w, diff_lambda=diff_lambda,
                  diff_norm_g=diff_norm_g, mla_q_norm_g=mla_q_norm_g, mla_kv_norm_g=mla_kv_norm_g,
                  mla_w_uq=mla_w_uq, mla_w_uk=mla_w_uk, mla_w_uv=mla_w_uv, w_out=w_out,
                  peer_w_q=peer_w_q, peer_sub_keys=peer_sub_keys, peer_u=peer_u, peer_v=peer_v)
    depth = ln1_g.shape[0]
    s = x_prompt.shape[1]
    ns = x_sample.shape[0]
    past_len = page_table.shape[1] * cache_diff_k.shape[2]
    tabs_p = _rope_tabs(jnp.arange(s))
    tabs_s = _rope_tabs(jnp.full((ns,), past_len, I32))
    gfin = final_norm_g.reshape(1, -1)
    caches = (cache_diff_k, cache_diff_v, cache_mla_ckv, cache_mla_kpe)

    xp, xs = x_prompt, x_sample.reshape(ns, -1)
    rows_p, rows_s = [], []
    for l in range(depth):
        lw = _prep_layer(l, params)
        final = l == depth - 1
        xp, new_p = _layer_prompt(xp, lw, tabs_p, gfin, final)
        xs3, new_s = _layer_sample(xs, lw, tabs_s, gfin, final, l, state_conv[l], caches, page_table)
        xs = xs3.reshape(ns, -1)
        rows_p.append(new_p)
        rows_s.append(new_s)
    outs_p = [jnp.stack(r) for r in zip(*rows_p)]
    outs_s = [jnp.stack(r) for r in zip(*rows_s)]
    return (xp, xs.reshape(ns, 1, -1), *outs_p, *outs_s)
```

```python
import functools
import math

import jax
import jax.numpy as jnp
from jax import lax
from jax.experimental import pallas as pl
from jax.experimental.pallas import tpu as pltpu

F32 = jnp.float32
BF16 = jnp.bfloat16
I32 = jnp.int32

D_MODEL = 1024
CONV_W = 256
CONV_K = 3
DIFF_HEADS = 4
DIFF_D = 32
DIFF_V = 64
DIFF_QK = 256
DIFF_W = 256
MLA_HEADS = 8
MLA_NOPE = 64
MLA_ROPE = 32
MLA_V = 64
MLA_KV = 256
MLA_Q = 384
MLA_W = 512
MLA_HEAD_PAD = 128
PEER_HEADS = 8
PEER_KEYS = 128
PEER_DKEY = 256
PEER_TOPK = 16
N_EXPERTS = PEER_KEYS * PEER_KEYS
PEER_SEL = PEER_HEADS * PEER_TOPK
ROPE_THETA = 10000.0
NORM_EPS = 1e-6
NEG_INF = -1e30
PAGE = 128
IN_W = 2208
IN_W_EXT = 2304
LANES = 128
SUBLANES = 8
VMEM_LIMIT = 56 * 1024 * 1024

DIFF_SCALE = DIFF_D ** -0.5
MLA_SCALE = (MLA_NOPE + MLA_ROPE) ** -0.5

_NT = (((1,), (1,)), ((), ()))


def _cparams(n_axes):
    return pltpu.CompilerParams(dimension_semantics=("arbitrary",) * n_axes,
                                vmem_limit_bytes=VMEM_LIMIT)


def _rms(x, g):
    ms = jnp.mean(x * x, axis=-1, keepdims=True)
    return x * lax.rsqrt(ms + NORM_EPS) * g


def _dot(a, b):
    return jnp.dot(a, b, preferred_element_type=F32)


def _dot_nt(a, b):
    return lax.dot_general(a, b, _NT, preferred_element_type=F32)


def _full_spec(shape):
    nd = len(shape)
    return pl.BlockSpec(shape, lambda *_: (0,) * nd)


def _proj_body(decode, tb, *refs):
    (x_ref, g1_ref, win_ref, cw_ref, qg_ref, kvg_ref, wq_ref, wqr_ref, wk_ref, wv_ref,
     cq_ref, ck_ref, sn_ref) = refs[:13]
    rest = refs[13:]
    if decode:
        s0_ref, s1_ref = rest[:2]
        rest = rest[2:]
    (y_ref, dqb_ref, dk_ref, dkb_ref, dv_ref, dvb_ref, ckv_ref, kpe_ref,
     qm_ref, km_ref, vm_ref, zo_ref) = rest[:12]

    x = x_ref[0]
    h = _rms(x, g1_ref[...]).astype(BF16)

    def proj(a, b):
        return _dot(h, win_ref[:, a:b])

    bg = proj(0, 256)
    z = proj(256, 512) * proj(512, 768)
    w0, w1, w2 = cw_ref[0:1, :], cw_ref[1:2, :], cw_ref[2:3, :]
    if decode:
        y = bg * (w0 * s0_ref[0] + w1 * s1_ref[0] + w2 * z)
        zo_ref[0] = z
    else:
        zs_ref = rest[12]
        j = pl.program_id(1)

        @pl.when(j == 0)
        def _():
            zs_ref[0:SUBLANES, :] = jnp.zeros((SUBLANES, CONV_W), F32)

        zs_ref[SUBLANES:SUBLANES + tb, :] = z
        zm1 = zs_ref[SUBLANES - 1:SUBLANES - 1 + tb, :]
        zm2 = zs_ref[SUBLANES - 2:SUBLANES - 2 + tb, :]
        y = bg * (w0 * zm2 + w1 * zm1 + w2 * z)
        zs_ref[0:SUBLANES, :] = z[tb - SUBLANES:tb, :]
        zo_ref[0] = z[tb - 2:tb, :]
    y_ref[0] = y.astype(BF16)

    dqb_ref[0] = (proj(768, 1024) * DIFF_SCALE).astype(BF16)
    dk = proj(1024, 1280)
    dk_ref[0] = dk
    dkb_ref[0] = dk.astype(BF16)
    dv = proj(1280, 1536)
    dv_ref[0] = dv
    dvb_ref[0] = dv.astype(BF16)

    cqn = _rms(proj(1536, 1920), qg_ref[...]).astype(BF16)
    cq8 = jnp.concatenate([cq_ref[0]] * MLA_HEADS, axis=1)
    sn8 = jnp.concatenate([sn_ref[0]] * MLA_HEADS, axis=1)
    qm = (_dot(cqn, wq_ref[...]) * cq8 + _dot(cqn, wqr_ref[...]) * sn8) * MLA_SCALE
    qm_ref[0] = qm.astype(BF16)

    ckvn = _rms(proj(1920, 2176), kvg_ref[...])
    ckv_ref[0] = ckvn
    ckb = ckvn.astype(BF16)
    tl = proj(2176, 2304)
    kr = tl * ck_ref[0] + pltpu.roll(tl, LANES - MLA_ROPE, 1) * sn_ref[0]
    kpe_ref[0] = kr[:, MLA_NOPE:MLA_NOPE + MLA_ROPE]
    km = _dot(ckb, wk_ref[...]) + jnp.concatenate([kr] * MLA_HEADS, axis=1)
    km_ref[0] = km.astype(BF16)
    vm_ref[0] = _dot(ckb, wv_ref[...]).astype(BF16)


def _proj_call(x3, lw, tabs, state=None):
    b, s, _ = x3.shape
    decode = state is not None
    tb = min(s, 512)
    assert s % tb == 0
    grid = (b, s // tb)

    def tok(w):
        return pl.BlockSpec((1, tb, w), lambda i, j: (i, j, 0))

    weights = [lw["g1"], lw["w_in"], lw["conv_w"], lw["qg"], lw["kvg"], lw["wq"], lw["wqr"], lw["wk"], lw["wv"]]
    tab = pl.BlockSpec((1, tb, LANES), lambda i, j: (0, j, 0))
    in_specs = [tok(D_MODEL)] + [_full_spec(w.shape) for w in weights] + [tab] * 3
    args = [x3] + weights + list(tabs)
    if decode:
        in_specs += [tok(CONV_W)] * 2
        args += list(state)

    def sds(w, dt):
        return jax.ShapeDtypeStruct((b, s, w), dt)

    out_shape = [sds(CONV_W, BF16), sds(DIFF_QK, BF16), sds(DIFF_QK, F32), sds(DIFF_QK, BF16),
                 sds(DIFF_W, F32), sds(DIFF_W, BF16), sds(MLA_KV, F32), sds(MLA_ROPE, F32),
                 sds(MLA_HEADS * MLA_HEAD_PAD, BF16), sds(MLA_HEADS * MLA_HEAD_PAD, BF16), sds(MLA_W, BF16)]
    out_specs = [tok(CONV_W), tok(DIFF_QK), tok(DIFF_QK), tok(DIFF_QK), tok(DIFF_W), tok(DIFF_W),
                 tok(MLA_KV), tok(MLA_ROPE), tok(MLA_HEADS * MLA_HEAD_PAD), tok(MLA_HEADS * MLA_HEAD_PAD),
                 tok(MLA_W)]
    scratch = []
    if decode:
        out_shape.append(sds(CONV_W, F32))
        out_specs.append(tok(CONV_W))
    else:
        out_shape.append(jax.ShapeDtypeStruct((b, CONV_K - 1, CONV_W), F32))
        out_specs.append(pl.BlockSpec((1, CONV_K - 1, CONV_W), lambda i, j: (i, 0, 0)))
        scratch.append(pltpu.VMEM((tb + SUBLANES, CONV_W), F32))
    return pl.pallas_call(
        functools.partial(_proj_body, decode, tb),
        grid=grid, in_specs=in_specs, out_specs=out_specs, out_shape=out_shape,
        scratch_shapes=scratch, compiler_params=_cparams(2), name="proj",
    )(*args)


def _flash_update(s, v, mask, m_ref, l_ref, acc_ref):
    if mask is not None:
        s = jnp.where(mask, s, NEG_INF)
    tiles = [s[:, c * LANES:(c + 1) * LANES] for c in range(s.shape[1] // LANES)]
    m_prev = m_ref[...]
    row_max = jnp.max(functools.reduce(jnp.maximum, tiles), axis=-1, keepdims=True)
    m_new = jnp.maximum(m_prev, row_max)
    alpha = jnp.exp(m_prev - m_new)
    p_tiles = [jnp.exp(t - m_new) for t in tiles]
    l_ref[...] = alpha * l_ref[...] + functools.reduce(jnp.add, p_tiles)
    p = jnp.concatenate(p_tiles, axis=1).astype(BF16)
    acc_ref[...] = alpha * acc_ref[...] + _dot(p, v)
    m_ref[...] = m_new


def _init_state(m_ref, l_ref, acc_ref):
    m_ref[...] = jnp.full(m_ref.shape, NEG_INF, F32)
    l_ref[...] = jnp.zeros(l_ref.shape, F32)
    acc_ref[...] = jnp.zeros(acc_ref.shape, F32)


def _causal_mask(rows, tq, tk, i, j):
    row = lax.broadcasted_iota(I32, (rows, tk), 0) & (tq - 1)
    col = lax.broadcasted_iota(I32, (rows, tk), 1)
    return col + j * tk <= row + i * tq


def _causal_sweep(tq, tk, i, scores, update, s_a, s_b):
    assert tk % tq == 0 and tq & (tq - 1) == 0
    j_diag = i // (tk // tq)
    n_pairs = j_diag // 2
    scores(0, s_a)

    def body(jj, c):
        j = 2 * jj
        scores(j + 1, s_b)
        update(s_a, j, False)
        scores(j + 2, s_a)
        update(s_b, j + 1, False)
        return c

    lax.fori_loop(0, n_pairs, body, 0)
    j0 = 2 * n_pairs

    @pl.when(j_diag == j0)
    def _():
        update(s_a, j0, True)

    @pl.when(j_diag != j0)
    def _():
        scores(j0 + 1, s_b)
        update(s_a, j0, False)
        update(s_b, j0 + 1, True)


def _diff_attn_body(tq, tk, lam_ref, q_ref, k_ref, v_ref, g_ref, o_ref, m_ref, l_ref, acc_ref, s_a, s_b):
    i = pl.program_id(2)
    q = q_ref[0].astype(F32)
    lane = lax.broadcasted_iota(I32, (1, LANES), 1)
    seg = lane >> 5
    qs = jnp.concatenate([jnp.where(seg == g, q, 0.0) for g in range(4)], axis=0).astype(BF16)
    _init_state(m_ref, l_ref, acc_ref)

    def scores(j, s_ref):
        s_ref[...] = _dot_nt(qs, k_ref[0, pl.ds(pl.multiple_of(j * tk, tk), tk), :])

    def update(s_ref, j, masked):
        v = v_ref[0, pl.ds(pl.multiple_of(j * tk, tk), tk), :]
        mask = _causal_mask(4 * tq, tq, tk, i, j) if masked else None
        _flash_update(s_ref[...], v, mask, m_ref, l_ref, acc_ref)

    _causal_sweep(tq, tk, i, scores, update, s_a, s_b)

    lam = lam_ref[0]
    o_all = acc_ref[...] / jnp.sum(l_ref[...], axis=-1, keepdims=True)
    o0 = o_all[0:tq] - lam * o_all[tq:2 * tq]
    o1 = o_all[2 * tq:3 * tq] - lam * o_all[3 * tq:4 * tq]
    first = lane < DIFF_V
    o = jnp.where(first, o0, o1)
    sq = o * o
    ss0 = jnp.sum(jnp.where(first, sq, 0.0), axis=-1, keepdims=True)
    ss1 = jnp.sum(jnp.where(first, 0.0, sq), axis=-1, keepdims=True)
    ms = jnp.where(first, ss0, ss1) * (1.0 / DIFF_V)
    o_ref[0] = (o * lax.rsqrt(ms + NORM_EPS) * g_ref[...]).astype(BF16)


def _diff_attn_call(dq, dk, dv, lam, gain, tq, tk):
    b, s, _ = dq.shape
    n_pairs = DIFF_QK // LANES
    grid = (b, n_pairs, s // tq)
    return pl.pallas_call(
        functools.partial(_diff_attn_body, tq, tk),
        grid=grid,
        in_specs=[pl.BlockSpec(memory_space=pltpu.SMEM),
                  pl.BlockSpec((1, tq, LANES), lambda bi, hp, i: (bi, i, hp)),
                  pl.BlockSpec((1, s, LANES), lambda bi, hp, i: (bi, 0, hp)),
                  pl.BlockSpec((1, s, LANES), lambda bi, hp, i: (bi, 0, hp)),
                  pl.BlockSpec((1, LANES), lambda bi, hp, i: (0, hp))],
        out_specs=pl.BlockSpec((1, tq, LANES), lambda bi, hp, i: (bi, i, hp)),
        out_shape=jax.ShapeDtypeStruct((b, s, DIFF_W), BF16),
        scratch_shapes=[pltpu.VMEM((4 * tq, LANES), F32)] * 3 + [pltpu.VMEM((4 * tq, tk), F32)] * 2,
        compiler_params=_cparams(3), name="diff_attn",
    )(lam, dq, dk, dv, gain)


def _mla_attn_body(tq, tk, q_ref, k_ref, v_ref, o_ref, m_ref, l_ref, acc_ref, s_a, s_b):
    i = pl.program_id(2)
    q = q_ref[0]
    _init_state(m_ref, l_ref, acc_ref)

    def scores(j, s_ref):
        k = k_ref[0, pl.ds(pl.multiple_of(j * tk, tk), tk), :]
        for hh in range(2):
            sl = slice(hh * MLA_HEAD_PAD, (hh + 1) * MLA_HEAD_PAD)
            s_ref[hh] = _dot_nt(q[:, sl], k[:, sl])

    def update(s_ref, j, masked):
        v = v_ref[0, pl.ds(pl.multiple_of(j * tk, tk), tk), :]
        mask = _causal_mask(tq, tq, tk, i, j) if masked else None
        for hh in range(2):
            _flash_update(s_ref[hh], v, mask, m_ref.at[hh], l_ref.at[hh], acc_ref.at[hh])

    _causal_sweep(tq, tk, i, scores, update, s_a, s_b)
    lane = lax.broadcasted_iota(I32, (1, LANES), 1)
    o = acc_ref[...] / jnp.sum(l_ref[...], axis=-1, keepdims=True)
    o_ref[0] = jnp.where(lane < MLA_V, o[0], o[1]).astype(BF16)


def _mla_attn_call(qm, km, vm, tq, tk):
    b, s, _ = qm.shape
    n_pairs = MLA_HEADS // 2
    grid = (b, n_pairs, s // tq)
    qw = 2 * MLA_HEAD_PAD
    return pl.pallas_call(
        functools.partial(_mla_attn_body, tq, tk),
        grid=grid,
        in_specs=[pl.BlockSpec((1, tq, qw), lambda bi, hp, i: (bi, i, hp)),
                  pl.BlockSpec((1, s, qw), lambda bi, hp, i: (bi, 0, hp)),
                  pl.BlockSpec((1, s, LANES), lambda bi, hp, i: (bi, 0, hp))],
        out_specs=pl.BlockSpec((1, tq, LANES), lambda bi, hp, i: (bi, i, hp)),
        out_shape=jax.ShapeDtypeStruct((b, s, MLA_W), BF16),
        scratch_shapes=[pltpu.VMEM((2, tq, LANES), F32)] * 3 + [pltpu.VMEM((2, tq, tk), F32)] * 2,
        compiler_params=_cparams(3), name="mla_attn",
    )(qm, km, vm)


def _decode_body(layer, sps, pps, n_pages, *refs):
    pt_ref = refs[0]
    (dq_ref, qm_ref, dks_ref, dvs_ref, ckvs_ref, kpes_ref, wk_ref, sel_ref, wuv_ref,
     gain_ref, lam_ref) = refs[1:12]
    cache_refs = refs[12:16]
    do_ref, mo_ref = refs[16:18]
    (qbd_ref, qlat_ref, qpe_ref, md_ref, ld_ref, accd_ref, mm_ref, lm_ref, accm_ref) = refs[18:27]
    bufs = refs[27:31]
    sem = refs[31]

    bi = pl.program_id(0)
    p = pl.program_id(1)
    n_steps = pl.num_programs(1)
    g = bi * n_steps + p
    n_rows = SUBLANES

    def page_copies(step_bi, step_p, slot, lookup):
        out = []
        for u in range(sps):
            for r in range(pps):
                page = pt_ref[(step_bi * sps + u) * n_pages + step_p * pps + r] if lookup else 0
                for c in range(4):
                    out.append(pltpu.make_async_copy(cache_refs[c].at[layer, page], bufs[c].at[slot, u * pps + r],
                                                     sem.at[slot, c]))
        return out

    @pl.when(g == 0)
    def _():
        for cp in page_copies(bi, p, 0, True):
            cp.start()

    @pl.when(g + 1 < pl.num_programs(0) * n_steps)
    def _():
        wrap = p == n_steps - 1
        for cp in page_copies(jnp.where(wrap, bi + 1, bi), jnp.where(wrap, 0, p + 1), (g + 1) & 1, True):
            cp.start()

    slot = g & 1
    for cp in page_copies(bi, p, slot, False):
        cp.wait()

    @pl.when(p == 0)
    def _():
        for u in range(sps):
            sub = lax.broadcasted_iota(I32, (n_rows, DIFF_QK), 0)
            lane = lax.broadcasted_iota(I32, (n_rows, DIFF_QK), 1)
            row = jnp.broadcast_to(dq_ref[u].astype(F32), (n_rows, DIFF_QK))
            qbd = jnp.where((lane >> 5) == sub, row, 0.0)
            qbd_ref[u] = qbd.astype(BF16)
            md_ref[u] = jnp.sum(qbd * dks_ref[u], axis=1, keepdims=True)
            ld_ref[u] = jnp.ones((n_rows, 1), F32)
            accd_ref[u] = jnp.broadcast_to(dvs_ref[u], (n_rows, DIFF_W))

            wq = MLA_HEADS * MLA_HEAD_PAD
            sub = lax.broadcasted_iota(I32, (n_rows, wq), 0)
            lane = lax.broadcasted_iota(I32, (n_rows, wq), 1)
            row = jnp.broadcast_to(qm_ref[u].astype(F32), (n_rows, wq))
            qf = jnp.where((lane >> 7) == sub, row, 0.0).astype(BF16)
            qlat = _dot_nt(qf, wk_ref[...]).astype(BF16)
            qpe = _dot(qf, sel_ref[...]).astype(BF16)
            qlat_ref[u] = qlat
            qpe_ref[u] = qpe
            mm_ref[u] = (jnp.sum(qlat.astype(F32) * ckvs_ref[u], axis=1, keepdims=True)
                         + jnp.sum(qpe.astype(F32) * kpes_ref[u], axis=1, keepdims=True))
            lm_ref[u] = jnp.ones((n_rows, 1), F32)
            accm_ref[u] = jnp.broadcast_to(ckvs_ref[u], (n_rows, MLA_KV))

    def rows(c, u):
        x = bufs[c][slot, u * pps:(u + 1) * pps]
        return x.reshape(pps * PAGE, x.shape[-1]).astype(BF16)

    def update(s, v, u, m_ref, l_ref, acc_ref):
        m_prev = m_ref[u]
        m_new = jnp.maximum(m_prev, jnp.max(s, axis=-1, keepdims=True))
        alpha = jnp.exp(m_prev - m_new)
        pr = jnp.exp(s - m_new)
        l_ref[u] = alpha * l_ref[u] + jnp.sum(pr, axis=-1, keepdims=True)
        acc_ref[u] = alpha * acc_ref[u] + _dot(pr.astype(BF16), v)
        m_ref[u] = m_new

    for u in range(sps):
        update(_dot_nt(qbd_ref[u], rows(0, u)), rows(1, u), u, md_ref, ld_ref, accd_ref)
        c = rows(2, u)
        kpe_t = jnp.concatenate([bufs[3][slot, u * pps + r] for r in range(pps)], axis=1).astype(BF16)
        update(_dot_nt(qlat_ref[u], c) + _dot(qpe_ref[u], kpe_t), c, u, mm_ref, lm_ref, accm_ref)

    @pl.when(p == n_steps - 1)
    def _():
        lam = lam_ref[...]
        for u in range(sps):
            sub = lax.broadcasted_iota(I32, (n_rows, DIFF_W), 0)
            lane = lax.broadcasted_iota(I32, (n_rows, DIFF_W), 1)
            coef = jnp.where((sub & 1) == 0, 1.0, -lam)
            o2 = accd_ref[u] / ld_ref[u] * coef
            orow = jnp.sum(jnp.where((lane >> 6) == (sub >> 1), o2, 0.0), axis=0, keepdims=True)
            lane1 = lane[0:1, :] >> 6
            sq = orow * orow
            ms = jnp.zeros_like(orow)
            for g in range(DIFF_HEADS):
                ssg = jnp.sum(jnp.where(lane1 == g, sq, 0.0), axis=-1, keepdims=True)
                ms = jnp.where(lane1 == g, ssg, ms)
            ms = ms * (1.0 / DIFF_V)
            do_ref[u] = (orow * lax.rsqrt(ms + NORM_EPS) * gain_ref[...]).astype(BF16)

            olat = (accm_ref[u] / lm_ref[u]).astype(BF16)
            r = _dot(olat, wuv_ref[...])
            sub = lax.broadcasted_iota(I32, (n_rows, MLA_W), 0)
            lane = lax.broadcasted_iota(I32, (n_rows, MLA_W), 1)
            mo_ref[u] = jnp.sum(jnp.where((lane >> 6) == sub, r, 0.0), axis=0, keepdims=True).astype(BF16)


def _decode_call(layer, page_table, caches, dq, qm, dks, dvs, ckvs, kpes, lw, sps, pps):
    ns, n_pages = page_table.shape
    assert n_pages % pps == 0 and ns % sps == 0
    n_steps = n_pages // pps
    cache_k, cache_v, cache_ckv, cache_kpe = caches
    cache_kpe_t = jnp.swapaxes(cache_kpe, 2, 3)

    def tok(w):
        return pl.BlockSpec((sps, 1, w), lambda bi, p, pt: (bi, 0, 0))

    def full(shape):
        nd = len(shape)
        return pl.BlockSpec(shape, lambda bi, p, pt: (0,) * nd)

    weights = [lw["wk"], lw["sel"], lw["wv"], lw["diff_gain"], lw["lam11"]]
    n_pg = sps * pps
    cache_args = [cache_k, cache_v, cache_ckv, cache_kpe_t]
    in_specs = ([tok(DIFF_QK), tok(MLA_HEADS * MLA_HEAD_PAD), tok(DIFF_QK), tok(DIFF_W), tok(MLA_KV), tok(MLA_ROPE)]
                + [full(w.shape) for w in weights]
                + [pl.BlockSpec(memory_space=pl.ANY)] * len(cache_args))
    args = [dq, qm, dks, dvs, ckvs, kpes] + weights + cache_args

    def state(w, dt=F32):
        return pltpu.VMEM((sps, SUBLANES, w), dt)

    n_slots = 2
    page_bufs = [pltpu.VMEM((n_slots, n_pg) + c.shape[2:], c.dtype) for c in cache_args]
    grid_spec = pltpu.PrefetchScalarGridSpec(
        num_scalar_prefetch=1, grid=(ns // sps, n_steps), in_specs=in_specs,
        out_specs=[tok(DIFF_W), tok(MLA_W)],
        scratch_shapes=[state(DIFF_QK, BF16), state(MLA_KV, BF16), state(MLA_ROPE, BF16),
                        state(1), state(1), state(DIFF_W), state(1), state(1), state(MLA_KV)]
        + page_bufs + [pltpu.SemaphoreType.DMA((n_slots, len(cache_args)))])
    return pl.pallas_call(
        functools.partial(_decode_body, layer, sps, pps, n_pages),
        grid_spec=grid_spec,
        out_shape=[jax.ShapeDtypeStruct((ns, 1, DIFF_W), BF16), jax.ShapeDtypeStruct((ns, 1, MLA_W), BF16)],
        compiler_params=_cparams(2), name="decode_attn",
    )(page_table.reshape(-1), *args)


def _post_body(x_ref, y_ref, d_ref, m_ref, wo_ref, g2_ref, wqh_ref, wql_ref, x1_ref, h2_ref, q_ref):
    x1 = (x_ref[...] + _dot(y_ref[...], wo_ref[0:CONV_W, :])
          + _dot(d_ref[...], wo_ref[CONV_W:CONV_W + DIFF_W, :])
          + _dot(m_ref[...], wo_ref[CONV_W + DIFF_W:, :]))
    x1_ref[...] = x1
    h2 = _rms(x1, g2_ref[...])
    hb = h2.astype(BF16)
    hl = (h2 - hb.astype(F32)).astype(BF16)
    q_ref[...] = _dot(hb, wqh_ref[...]) + _dot(hb, wql_ref[...]) + _dot(hl, wqh_ref[...])
    h2_ref[...] = hb


def _post_call(x, y, d, m, lw):
    t = x.shape[0]
    tb = min(t, 256)
    qw = PEER_HEADS * PEER_DKEY

    def tok(w):
        return pl.BlockSpec((tb, w), lambda i: (i, 0))

    weights = [lw["w_out"], lw["g2"], lw["wq_hi"], lw["wq_lo"]]
    return pl.pallas_call(
        _post_body, grid=(t // tb,),
        in_specs=[tok(D_MODEL), tok(CONV_W), tok(DIFF_W), tok(MLA_W)] + [_full_spec(w.shape) for w in weights],
        out_specs=[tok(D_MODEL), tok(D_MODEL), tok(qw)],
        out_shape=[jax.ShapeDtypeStruct((t, D_MODEL), F32), jax.ShapeDtypeStruct((t, D_MODEL), BF16),
                   jax.ShapeDtypeStruct((t, qw), F32)],
        compiler_params=_cparams(1), name="post_attn",
    )(x, y, d, m, *weights)


_CAND_PAIRS = [(a, b_) for a in range(PEER_TOPK) for b_ in range(PEER_TOPK) if (a + 1) * (b_ + 1) <= PEER_TOPK]
_CAND_ROWS = -(-len(_CAND_PAIRS) // SUBLANES) * SUBLANES


def _cand_tables():
    n_pad = _CAND_ROWS - len(_CAND_PAIRS)
    flat = [a * PEER_TOPK + b_ for a, b_ in _CAND_PAIRS] + [PEER_TOPK * PEER_TOPK] * n_pad
    pen = [0.0] * len(_CAND_PAIRS) + [-math.inf] * n_pad
    rows = jnp.arange(PEER_TOPK)[None, :]
    pick_a = (jnp.asarray([a for a, _ in _CAND_PAIRS] + [0] * n_pad)[:, None] == rows).astype(BF16)
    pick_b = (jnp.asarray([b_ for _, b_ in _CAND_PAIRS] + [0] * n_pad)[:, None] == rows).astype(BF16)
    return (jnp.asarray(flat, I32).reshape(-1, 1), jnp.asarray(pen, F32).reshape(-1, 1), pick_a, pick_b)


def _pick_rows(onehot, x):
    hi = x.astype(BF16)
    r1 = x - hi.astype(F32)
    mid = r1.astype(BF16)
    lo = (r1 - mid.astype(F32)).astype(BF16)
    return (_dot(onehot, hi) + _dot(onehot, mid)) + _dot(onehot, lo)


_ROUTE_HEADS = 4


def _route_body(tb, q_ref, khi_ref, klo_ref, flat_ref, pen_ref, pa_ref, pb_ref, io_ref, jo_ref, go_ref,
                sv_ref, si_ref, i_scr, j_scr, g_scr):
    hg = pl.program_id(1)
    q = q_ref[...]
    qh = q.astype(BF16)
    ql = (q - qh.astype(F32)).astype(BF16)
    n_iota = lax.broadcasted_iota(I32, (PEER_KEYS, tb), 0)

    for u in range(_ROUTE_HEADS):
        def sub_scores(p, u=u):
            c0 = (2 * u + p) * PEER_KEYS
            kh, kl = khi_ref[u, p], klo_ref[u, p]
            qh_, ql_ = qh[:, c0:c0 + PEER_KEYS], ql[:, c0:c0 + PEER_KEYS]
            return _dot_nt(kh, qh_) + _dot_nt(kh, ql_) + _dot_nt(kl, qh_)

        def it(k, ss, u=u):
            out = []
            for p, s in enumerate(ss):
                m = jnp.max(s, axis=0, keepdims=True)
                idx = jnp.min(jnp.where(s == m, n_iota, PEER_KEYS), axis=0, keepdims=True)
                sv_ref[u, p, pl.ds(k, 1), :] = m
                si_ref[u, p, pl.ds(k, 1), :] = idx
                out.append(jnp.where(n_iota == idx, -jnp.inf, s))
            return tuple(out)

        lax.fori_loop(0, PEER_TOPK, it, (sub_scores(0), sub_scores(1)))

    cands = tuple((_pick_rows(pa_ref[...], sv_ref[u, 0]) + _pick_rows(pb_ref[...], sv_ref[u, 1])) + pen_ref[...]
                  for u in range(_ROUTE_HEADS))
    flat = jnp.broadcast_to(flat_ref[...], (_CAND_ROWS, tb))
    k_iota = lax.broadcasted_iota(I32, (PEER_TOPK, tb), 0)
    base = pl.multiple_of(hg * (_ROUTE_HEADS * PEER_TOPK), _ROUTE_HEADS * PEER_TOPK)

    def it2(k, cs):
        out = []
        for u, cand in enumerate(cs):
            row = base + u * PEER_TOPK + k
            m = jnp.max(cand, axis=0, keepdims=True)
            fl = jnp.min(jnp.where(cand == m, flat, PEER_TOPK * PEER_TOPK), axis=0, keepdims=True)
            a = fl >> 4
            b_ = fl & (PEER_TOPK - 1)
            g_scr[pl.ds(row, 1), :] = m
            i_scr[pl.ds(row, 1), :] = jnp.sum(jnp.where(k_iota == a, si_ref[u, 0], 0), axis=0, keepdims=True)
            j_scr[pl.ds(row, 1), :] = jnp.sum(jnp.where(k_iota == b_, si_ref[u, 1], 0), axis=0, keepdims=True)
            out.append(jnp.where(flat == fl, -jnp.inf, cand))
        return tuple(out)

    lax.fori_loop(0, PEER_TOPK, it2, cands)
    for u in range(_ROUTE_HEADS):
        rows = pl.ds(base + u * PEER_TOPK, PEER_TOPK)
        ts = g_scr[rows, :]
        e = jnp.exp(ts - ts[0:1, :])
        g_scr[rows, :] = e / jnp.sum(e, axis=0, keepdims=True)

    @pl.when(hg == pl.num_programs(1) - 1)
    def _():
        io_ref[...] = i_scr[...].T
        jo_ref[...] = j_scr[...].T
        go_ref[...] = g_scr[...].T


def _route_call(q, lw):
    t = q.shape[0]
    tb = min(t, 256)
    tables = _cand_tables()
    tokq = pl.BlockSpec((tb, _ROUTE_HEADS * PEER_DKEY), lambda i, h: (i, h))
    keys = pl.BlockSpec((_ROUTE_HEADS, 2, PEER_KEYS, PEER_DKEY // 2), lambda i, h: (h, 0, 0, 0))
    out = pl.BlockSpec((tb, PEER_SEL), lambda i, h: (i, 0))
    return pl.pallas_call(
        functools.partial(_route_body, tb), grid=(t // tb, PEER_HEADS // _ROUTE_HEADS),
        in_specs=[tokq, keys, keys] + [pl.BlockSpec(tb_.shape, lambda i, h: (0, 0)) for tb_ in tables],
        out_specs=[out, out, out],
        out_shape=[jax.ShapeDtypeStruct((t, PEER_SEL), I32), jax.ShapeDtypeStruct((t, PEER_SEL), I32),
                   jax.ShapeDtypeStruct((t, PEER_SEL), F32)],
        scratch_shapes=[pltpu.VMEM((_ROUTE_HEADS, 2, PEER_TOPK, tb), F32),
                        pltpu.VMEM((_ROUTE_HEADS, 2, PEER_TOPK, tb), I32),
                        pltpu.VMEM((PEER_SEL, tb), I32), pltpu.VMEM((PEER_SEL, tb), I32),
                        pltpu.VMEM((PEER_SEL, tb), F32)],
        compiler_params=_cparams(2), name="peer_route",
    )(q, lw["keys_hi"], lw["keys_lo"], *tables)


_CHUNK_ROWS = 32
_CHUNK = _CHUNK_ROWS * PEER_KEYS


def _up_body(h_ref, u_ref, i_ref, j_ref, g_ref, o_ref, val_ref):
    c = pl.program_id(1)

    @pl.when(c == 0)
    def _():
        val_ref[...] = jnp.zeros(val_ref.shape, F32)

    a = _dot_nt(h_ref[...], u_ref[...])
    iv, jv = i_ref[...], j_ref[...]
    acc = val_ref[...]
    for r in range(_CHUNK_ROWS):
        got = jnp.take_along_axis(a[:, r * PEER_KEYS:(r + 1) * PEER_KEYS], jv, axis=1)
        acc = acc + jnp.where(iv == c * _CHUNK_ROWS + r, got, 0.0)
    val_ref[...] = acc

    @pl.when(c == pl.num_programs(1) - 1)
    def _():
        o_ref[...] = g_ref[...] * (0.5 * acc * (1.0 + lax.erf(acc * (2.0 ** -0.5))))


def _up_call(h2, i_idx, j_idx, gate, lw):
    t = h2.shape[0]
    tb = min(t, 512)
    sel = pl.BlockSpec((tb, PEER_SEL), lambda i, c: (i, 0))
    return pl.pallas_call(
        _up_body, grid=(t // tb, N_EXPERTS // _CHUNK),
        in_specs=[pl.BlockSpec((tb, D_MODEL), lambda i, c: (i, 0)),
                  pl.BlockSpec((_CHUNK, D_MODEL), lambda i, c: (c, 0)), sel, sel, sel],
        out_specs=sel,
        out_shape=jax.ShapeDtypeStruct((t, PEER_SEL), F32),
        scratch_shapes=[pltpu.VMEM((tb, PEER_SEL), F32)],
        compiler_params=_cparams(2), name="peer_up",
    )(h2, lw["u"], i_idx, j_idx, gate)


_Y_PITCH = PEER_KEYS + SUBLANES


def _down_body(final, tb, x_ref, w_ref, i_ref, j_ref, v_ref, gf_ref, o_ref, y_scr, acc_ref):
    c = pl.program_id(1)

    @pl.when(c == 0)
    def _():
        acc_ref[...] = jnp.zeros(acc_ref.shape, F32)
        shape = (PEER_KEYS, PEER_SEL)
        sub = lax.broadcasted_iota(I32, shape, 0)

        def token(t, carry):
            wrow = jnp.broadcast_to(w_ref[pl.ds(t, 1), :], shape)
            irow = jnp.broadcast_to(i_ref[pl.ds(t, 1), :], shape)
            jrow = jnp.broadcast_to(j_ref[pl.ds(t, 1), :], shape)
            lhs = jnp.where(irow == sub, wrow, 0.0).astype(BF16)
            rhs = jnp.where(jrow == sub, 1.0, 0.0).astype(BF16)
            y_scr[pl.ds(pl.multiple_of(t * _Y_PITCH, SUBLANES), PEER_KEYS), :] = _dot_nt(lhs, rhs)
            return carry

        lax.fori_loop(0, tb, token, 0, unroll=64)

    base = c * _CHUNK_ROWS
    lhs = jnp.concatenate([y_scr[pl.ds(base + r, tb, stride=_Y_PITCH), :] for r in range(_CHUNK_ROWS)],
                          axis=1).astype(BF16)
    acc_ref[...] += _dot(lhs, v_ref[...])

    @pl.when(c == pl.num_programs(1) - 1)
    def _():
        xn = x_ref[...] + acc_ref[...]
        if final:
            xn = _rms(xn, gf_ref[...])
        o_ref[...] = xn


def _down_call(x1, wgt, i_idx, j_idx, lw, gfin, final):
    t = x1.shape[0]
    tb = min(t, 256)
    sel = pl.BlockSpec((tb, PEER_SEL), lambda i, c: (i, 0))
    tok = pl.BlockSpec((tb, D_MODEL), lambda i, c: (i, 0))
    return pl.pallas_call(
        functools.partial(_down_body, final, tb), grid=(t // tb, N_EXPERTS // _CHUNK),
        in_specs=[tok, sel, sel, sel, pl.BlockSpec((_CHUNK, D_MODEL), lambda i, c: (c, 0)),
                  pl.BlockSpec((1, D_MODEL), lambda i, c: (0, 0))],
        out_specs=tok,
        out_shape=jax.ShapeDtypeStruct((t, D_MODEL), F32),
        scratch_shapes=[pltpu.VMEM((tb * _Y_PITCH, PEER_KEYS), F32), pltpu.VMEM((tb, D_MODEL), F32)],
        compiler_params=_cparams(2), name="peer_down",
    )(x1, wgt, i_idx, j_idx, lw["v"], gfin)


def _split_bf16(w):
    hi = w.astype(BF16)
    return hi, (w - hi.astype(F32)).astype(BF16)


def _rot_half(w):
    half = MLA_ROPE // 2
    return jnp.concatenate([-w[..., half:], w[..., :half]], axis=-1)


def _prep_layer(l, p):
    w_in = p["w_in"][l]
    kpe_w = w_in[:, IN_W - MLA_ROPE:]
    w_in_ext = jnp.concatenate(
        [w_in[:, :IN_W - MLA_ROPE], jnp.zeros((D_MODEL, MLA_NOPE), F32), kpe_w, _rot_half(kpe_w)], axis=1)
    w_uq = p["mla_w_uq"][l]
    pad = jnp.zeros((MLA_Q, MLA_HEADS, MLA_HEAD_PAD - MLA_NOPE - MLA_ROPE), F32)
    wq = jnp.concatenate([w_uq, pad], axis=-1).reshape(MLA_Q, -1)
    wqr = jnp.concatenate([jnp.zeros((MLA_Q, MLA_HEADS, MLA_NOPE), F32), _rot_half(w_uq[..., MLA_NOPE:]), pad],
                          axis=-1).reshape(MLA_Q, -1)
    w_uk = p["mla_w_uk"][l]
    wk = jnp.concatenate([w_uk, jnp.zeros((MLA_KV, MLA_HEADS, MLA_HEAD_PAD - MLA_NOPE), F32)],
                         axis=-1).reshape(MLA_KV, -1)
    rows = jnp.arange(MLA_HEADS * MLA_HEAD_PAD)
    sel = ((rows[:, None] % MLA_HEAD_PAD) == (MLA_NOPE + jnp.arange(MLA_ROPE))[None, :]).astype(BF16)
    lp = p["diff_lambda"][l].astype(F32)
    lam_init = 0.8 - 0.6 * math.exp(-0.3 * l)
    lam = jnp.exp(jnp.sum(lp[0] * lp[1])) - jnp.exp(jnp.sum(lp[2] * lp[3])) + lam_init
    wq_hi, wq_lo = _split_bf16(p["peer_w_q"][l])
    keys_hi, keys_lo = _split_bf16(p["peer_sub_keys"][l])
    return {
        "g1": p["ln1_g"][l].reshape(1, -1), "g2": p["ln2_g"][l].reshape(1, -1),
        "w_in": w_in_ext.astype(BF16), "conv_w": p["conv_w"][l],
        "qg": p["mla_q_norm_g"][l].reshape(1, -1), "kvg": p["mla_kv_norm_g"][l].reshape(1, -1),
        "wq": wq.astype(BF16), "wqr": wqr.astype(BF16), "wk": wk.astype(BF16),
        "wv": p["mla_w_uv"][l].reshape(MLA_KV, MLA_W).astype(BF16), "sel": sel,
        "lam": lam.reshape(1), "lam11": lam.reshape(1, 1),
        "diff_gain": (jnp.tile(p["diff_norm_g"][l], DIFF_HEADS) * (1.0 - lam_init)).reshape(1, -1),
        "w_out": p["w_out"][l].astype(BF16), "wq_hi": wq_hi, "wq_lo": wq_lo,
        "keys_hi": keys_hi, "keys_lo": keys_lo,
        "u": p["peer_u"][l].astype(BF16), "v": p["peer_v"][l].astype(BF16),
    }


def _rope_tabs(pos):
    half = MLA_ROPE // 2
    inv = ROPE_THETA ** (-jnp.arange(half, dtype=F32) * (2.0 / MLA_ROPE))
    ang = pos.astype(F32)[:, None] * inv[None, :]
    cos2 = jnp.concatenate([jnp.cos(ang)] * 2, axis=-1)
    sin2 = jnp.concatenate([jnp.sin(ang)] * 2, axis=-1)
    n = pos.shape[0]
    z64 = jnp.zeros((n, MLA_NOPE), F32)
    z32 = jnp.zeros((n, LANES - MLA_NOPE - MLA_ROPE), F32)
    cq = jnp.concatenate([jnp.ones((n, MLA_NOPE), F32), cos2, z32], axis=1)
    ck = jnp.concatenate([z64, cos2, z32], axis=1)
    sn = jnp.concatenate([z64, sin2, z32], axis=1)
    return cq, ck, sn


def _peer_and_residual(x1, h2, q, lw, gfin, final):
    i_idx, j_idx, gate = _route_call(q, lw)
    wgt = _up_call(h2, i_idx, j_idx, gate, lw)
    return _down_call(x1, wgt, i_idx, j_idx, lw, gfin, final)


def _layer_prompt(x, lw, tabs, gfin, final, tq_diff=256, tq_mla=512, tk=512):
    b, s, _ = x.shape
    tabs3 = [t[None] for t in tabs]
    (y, dqb, dk, dkb, dv, dvb, ckv, kpe, qm, km, vm, newconv) = _proj_call(x, lw, tabs3)
    tk = min(tk, s)
    d_out = _diff_attn_call(dqb, dkb, dvb, lw["lam"], lw["diff_gain"], min(tq_diff, s), tk)
    m_out = _mla_attn_call(qm, km, vm, min(tq_mla, s), tk)
    t = b * s
    x1, h2, q = _post_call(x.reshape(t, -1), y.reshape(t, -1), d_out.reshape(t, -1), m_out.reshape(t, -1), lw)
    xn = _peer_and_residual(x1, h2, q, lw, gfin, final)
    return xn.reshape(b, s, -1), (newconv, dk, dv, ckv, kpe)


def _layer_sample(x, lw, tabs, gfin, final, layer, state, caches, page_table, sps=2, pps=16):
    ns = x.shape[0]
    x3 = x.reshape(1, ns, -1)
    tabs3 = [t[None] for t in tabs]
    st = (state[:, 0, :][None], state[:, 1, :][None])
    (y, dqb, dk, dkb, dv, dvb, ckv, kpe, qm, km, vm, z) = _proj_call(x3, lw, tabs3, st)
    del dkb, dvb, km, vm

    def per_tok(a):
        return a.reshape(ns, 1, -1)

    d_out, m_out = _decode_call(layer, page_table, caches, per_tok(dqb), per_tok(qm), per_tok(dk), per_tok(dv),
                                per_tok(ckv), per_tok(kpe), lw, sps, min(pps, page_table.shape[1]))
    x1, h2, q = _post_call(x.reshape(ns, -1), y.reshape(ns, -1), d_out.reshape(ns, -1), m_out.reshape(ns, -1), lw)
    xn = _peer_and_residual(x1, h2, q, lw, gfin, final)
    newconv = jnp.stack([state[:, 1, :], z.reshape(ns, -1)], axis=1)
    return xn.reshape(ns, 1, -1), (newconv, per_tok(dk), per_tok(dv), per_tok(ckv), per_tok(kpe))


def kernel(x_prompt, x_sample, state_conv, cache_diff_k, cache_diff_v, cache_mla_ckv, cache_mla_kpe,
           page_table, ln1_g, ln2_g, w_in, conv_w, diff_lambda, diff_norm_g, mla_q_norm_g, mla_kv_norm_g,
           mla_w_uq, mla_w_uk, mla_w_uv, w_out, peer_w_q, peer_sub_keys, peer_u, peer_v, final_norm_g):
    params = dict(ln1_g=ln1_g, ln2_g=ln2_g, w_in=w_in, conv_w=conv_w, diff_lambda=diff_lambda,
                  diff_norm_g=diff_norm_g, mla_q_norm_g=mla_q_norm_g, mla_kv_norm_g=mla_kv_norm_g,
                  mla_w_uq=mla_w_uq, mla_w_uk=mla_w_uk, mla_w_uv=mla_w_uv, w_out=w_out,
                  peer_w_q=peer_w_q, peer_sub_keys=peer_sub_keys, peer_u=peer_u, peer_v=peer_v)
    depth = ln1_g.shape[0]
    s = x_prompt.shape[1]
    ns = x_sample.shape[0]
    past_len = page_table.shape[1] * cache_diff_k.shape[2]
    tabs_p = _rope_tabs(jnp.arange(s))
    tabs_s = _rope_tabs(jnp.full((ns,), past_len, I32))
    gfin = final_norm_g.reshape(1, -1)
    caches = (cache_diff_k, cache_diff_v, cache_mla_ckv, cache_mla_kpe)

    xp, xs = x_prompt, x_sample.reshape(ns, -1)
    rows_p, rows_s = [], []
    for l in range(depth):
        lw = _prep_layer(l, params)
        final = l == depth - 1
        xp, new_p = _layer_prompt(xp, lw, tabs_p, gfin, final)
        xs3, new_s = _layer_sample(xs, lw, tabs_s, gfin, final, l, state_conv[l], caches, page_table)
        xs = xs3.reshape(ns, -1)
        rows_p.append(new_p)
        rows_s.append(new_s)
    outs_p = [jnp.stack(r) for r in zip(*rows_p)]
    outs_s = [jnp.stack(r) for r in zip(*rows_s)]
    return (xp, xs.reshape(ns, 1, -1), *outs_p, *outs_s)
```

```python
import functools
import math

import jax
import jax.numpy as jnp
from jax import lax
from jax.experimental import pallas as pl
from jax.experimental.pallas import tpu as pltpu

F32 = jnp.float32
BF16 = jnp.bfloat16
I32 = jnp.int32

D_MODEL = 1024
CONV_W = 256
CONV_K = 3
DIFF_HEADS = 4
DIFF_D = 32
DIFF_V = 64
DIFF_QK = 256
DIFF_W = 256
MLA_HEADS = 8
MLA_NOPE = 64
MLA_ROPE = 32
MLA_V = 64
MLA_KV = 256
MLA_Q = 384
MLA_W = 512
MLA_HEAD_PAD = 128
PEER_HEADS = 8
PEER_KEYS = 128
PEER_DKEY = 256
PEER_TOPK = 16
N_EXPERTS = PEER_KEYS * PEER_KEYS
PEER_SEL = PEER_HEADS * PEER_TOPK
ROPE_THETA = 10000.0
NORM_EPS = 1e-6
NEG_INF = -1e30
PAGE = 128
IN_W = 2208
IN_W_EXT = 2304
LANES = 128
SUBLANES = 8
VMEM_LIMIT = 56 * 1024 * 1024

DIFF_SCALE = DIFF_D ** -0.5
MLA_SCALE = (MLA_NOPE + MLA_ROPE) ** -0.5

_NT = (((1,), (1,)), ((), ()))


def _cparams(n_axes):
    return pltpu.CompilerParams(dimension_semantics=("arbitrary",) * n_axes,
                                vmem_limit_bytes=VMEM_LIMIT)


def _rms(x, g):
    ms = jnp.mean(x * x, axis=-1, keepdims=True)
    return x * lax.rsqrt(ms + NORM_EPS) * g


def _dot(a, b):
    return jnp.dot(a, b, preferred_element_type=F32)


def _dot_nt(a, b):
    return lax.dot_general(a, b, _NT, preferred_element_type=F32)


def _full_spec(shape):
    nd = len(shape)
    return pl.BlockSpec(shape, lambda *_: (0,) * nd)


def _proj_body(decode, tb, *refs):
    (x_ref, g1_ref, win_ref, cw_ref, qg_ref, kvg_ref, wq_ref, wqr_ref, wk_ref, wv_ref,
     cq_ref, ck_ref, sn_ref) = refs[:13]
    rest = refs[13:]
    if decode:
        s0_ref, s1_ref = rest[:2]
        rest = rest[2:]
    (y_ref, dqb_ref, dk_ref, dkb_ref, dv_ref, dvb_ref, ckv_ref, kpe_ref,
     qm_ref, km_ref, vm_ref, zo_ref) = rest[:12]

    x = x_ref[0]
    h = _rms(x, g1_ref[...]).astype(BF16)

    def proj(a, b):
        return _dot(h, win_ref[:, a:b])

    bg = proj(0, 256)
    z = proj(256, 512) * proj(512, 768)
    w0, w1, w2 = cw_ref[0:1, :], cw_ref[1:2, :], cw_ref[2:3, :]
    if decode:
        y = bg * (w0 * s0_ref[0] + w1 * s1_ref[0] + w2 * z)
        zo_ref[0] = z
    else:
        zs_ref = rest[12]
        j = pl.program_id(1)

        @pl.when(j == 0)
        def _():
            zs_ref[0:SUBLANES, :] = jnp.zeros((SUBLANES, CONV_W), F32)

        zs_ref[SUBLANES:SUBLANES + tb, :] = z
        zm1 = zs_ref[SUBLANES - 1:SUBLANES - 1 + tb, :]
        zm2 = zs_ref[SUBLANES - 2:SUBLANES - 2 + tb, :]
        y = bg * (w0 * zm2 + w1 * zm1 + w2 * z)
        zs_ref[0:SUBLANES, :] = z[tb - SUBLANES:tb, :]
        zo_ref[0] = z[tb - 2:tb, :]
    y_ref[0] = y.astype(BF16)

    dqb_ref[0] = (proj(768, 1024) * DIFF_SCALE).astype(BF16)
    dk = proj(1024, 1280)
    dk_ref[0] = dk
    dkb_ref[0] = dk.astype(BF16)
    dv = proj(1280, 1536)
    dv_ref[0] = dv
    dvb_ref[0] = dv.astype(BF16)

    cqn = _rms(proj(1536, 1920), qg_ref[...]).astype(BF16)
    cq8 = jnp.concatenate([cq_ref[0]] * MLA_HEADS, axis=1)
    sn8 = jnp.concatenate([sn_ref[0]] * MLA_HEADS, axis=1)
    qm = (_dot(cqn, wq_ref[...]) * cq8 + _dot(cqn, wqr_ref[...]) * sn8) * MLA_SCALE
    qm_ref[0] = qm.astype(BF16)

    ckvn = _rms(proj(1920, 2176), kvg_ref[...])
    ckv_ref[0] = ckvn
    ckb = ckvn.astype(BF16)
    tl = proj(2176, 2304)
    kr = tl * ck_ref[0] + pltpu.roll(tl, LANES - MLA_ROPE, 1) * sn_ref[0]
    kpe_ref[0] = kr[:, MLA_NOPE:MLA_NOPE + MLA_ROPE]
    km = _dot(ckb, wk_ref[...]) + jnp.concatenate([kr] * MLA_HEADS, axis=1)
    km_ref[0] = km.astype(BF16)
    vm_ref[0] = _dot(ckb, wv_ref[...]).astype(BF16)


def _proj_call(x3, lw, tabs, state=None):
    b, s, _ = x3.shape
    decode = state is not None
    tb = min(s, 512)
    assert s % tb == 0
    grid = (b, s // tb)

    def tok(w):
        return pl.BlockSpec((1, tb, w), lambda i, j: (i, j, 0))

    weights = [lw["g1"], lw["w_in"], lw["conv_w"], lw["qg"], lw["kvg"], lw["wq"], lw["wqr"], lw["wk"], lw["wv"]]
    tab = pl.BlockSpec((1, tb, LANES), lambda i, j: (0, j, 0))
    in_specs = [tok(D_MODEL)] + [_full_spec(w.shape) for w in weights] + [tab] * 3
    args = [x3] + weights + list(tabs)
    if decode:
        in_specs += [tok(CONV_W)] * 2
        args += list(state)

    def sds(w, dt):
        return jax.ShapeDtypeStruct((b, s, w), dt)

    out_shape = [sds(CONV_W, BF16), sds(DIFF_QK, BF16), sds(DIFF_QK, F32), sds(DIFF_QK, BF16),
                 sds(DIFF_W, F32), sds(DIFF_W, BF16), sds(MLA_KV, F32), sds(MLA_ROPE, F32),
                 sds(MLA_HEADS * MLA_HEAD_PAD, BF16), sds(MLA_HEADS * MLA_HEAD_PAD, BF16), sds(MLA_W, BF16)]
    out_specs = [tok(CONV_W), tok(DIFF_QK), tok(DIFF_QK), tok(DIFF_QK), tok(DIFF_W), tok(DIFF_W),
                 tok(MLA_KV), tok(MLA_ROPE), tok(MLA_HEADS * MLA_HEAD_PAD), tok(MLA_HEADS * MLA_HEAD_PAD),
                 tok(MLA_W)]
    scratch = []
    if decode:
        out_shape.append(sds(CONV_W, F32))
        out_specs.append(tok(CONV_W))
    else:
        out_shape.append(jax.ShapeDtypeStruct((b, CONV_K - 1, CONV_W), F32))
        out_specs.append(pl.BlockSpec((1, CONV_K - 1, CONV_W), lambda i, j: (i, 0, 0)))
        scratch.append(pltpu.VMEM((tb + SUBLANES, CONV_W), F32))
    return pl.pallas_call(
        functools.partial(_proj_body, decode, tb),
        grid=grid, in_specs=in_specs, out_specs=out_specs, out_shape=out_shape,
        scratch_shapes=scratch, compiler_params=_cparams(2), name="proj",
    )(*args)


def _flash_update(s, v, mask, m_ref, l_ref, acc_ref):
    if mask is not None:
        s = jnp.where(mask, s, NEG_INF)
    tiles = [s[:, c * LANES:(c + 1) * LANES] for c in range(s.shape[1] // LANES)]
    m_prev = m_ref[...]
    row_max = jnp.max(functools.reduce(jnp.maximum, tiles), axis=-1, keepdims=True)
    m_new = jnp.maximum(m_prev, row_max)
    alpha = jnp.exp(m_prev - m_new)
    p_tiles = [jnp.exp(t - m_new) for t in tiles]
    l_ref[...] = alpha * l_ref[...] + functools.reduce(jnp.add, p_tiles)
    p = jnp.concatenate(p_tiles, axis=1).astype(BF16)
    acc_ref[...] = alpha * acc_ref[...] + _dot(p, v)
    m_ref[...] = m_new


def _init_state(m_ref, l_ref, acc_ref):
    m_ref[...] = jnp.full(m_ref.shape, NEG_INF, F32)
    l_ref[...] = jnp.zeros(l_ref.shape, F32)
    acc_ref[...] = jnp.zeros(acc_ref.shape, F32)


def _causal_mask(rows, tq, tk, i, j):
    row = lax.broadcasted_iota(I32, (rows, tk), 0) & (tq - 1)
    col = lax.broadcasted_iota(I32, (rows, tk), 1)
    return col + j * tk <= row + i * tq


def _causal_sweep(tq, tk, i, scores, update, s_a, s_b):
    assert tk % tq == 0 and tq & (tq - 1) == 0
    j_diag = i // (tk // tq)
    n_pairs = j_diag // 2
    scores(0, s_a)

    def body(jj, c):
        j = 2 * jj
        scores(j + 1, s_b)
        update(s_a, j, False)
        scores(j + 2, s_a)
        update(s_b, j + 1, False)
        return c

    lax.fori_loop(0, n_pairs, body, 0)
    j0 = 2 * n_pairs

    @pl.when(j_diag == j0)
    def _():
        update(s_a, j0, True)

    @pl.when(j_diag != j0)
    def _():
        scores(j0 + 1, s_b)
        update(s_a, j0, False)
        update(s_b, j0 + 1, True)


def _diff_attn_body(tq, tk, lam_ref, q_ref, k_ref, v_ref, g_ref, o_ref, m_ref, l_ref, acc_ref, s_a, s_b):
    i = pl.program_id(2)
    q = q_ref[0].astype(F32)
    lane = lax.broadcasted_iota(I32, (1, LANES), 1)
    seg = lane >> 5
    qs = jnp.concatenate([jnp.where(seg == g, q, 0.0) for g in range(4)], axis=0).astype(BF16)
    _init_state(m_ref, l_ref, acc_ref)

    def scores(j, s_ref):
        s_ref[...] = _dot_nt(qs, k_ref[0, pl.ds(pl.multiple_of(j * tk, tk), tk), :])

    def update(s_ref, j, masked):
        v = v_ref[0, pl.ds(pl.multiple_of(j * tk, tk), tk), :]
        mask = _causal_mask(4 * tq, tq, tk, i, j) if masked else None
        _flash_update(s_ref[...], v, mask, m_ref, l_ref, acc_ref)

    _causal_sweep(tq, tk, i, scores, update, s_a, s_b)

    lam = lam_ref[0]
    o_all = acc_ref[...] / jnp.sum(l_ref[...], axis=-1, keepdims=True)
    o0 = o_all[0:tq] - lam * o_all[tq:2 * tq]
    o1 = o_all[2 * tq:3 * tq] - lam * o_all[3 * tq:4 * tq]
    first = lane < DIFF_V
    o = jnp.where(first, o0, o1)
    sq = o * o
    ss0 = jnp.sum(jnp.where(first, sq, 0.0), axis=-1, keepdims=True)
    ss1 = jnp.sum(jnp.where(first, 0.0, sq), axis=-1, keepdims=True)
    ms = jnp.where(first, ss0, ss1) * (1.0 / DIFF_V)
    o_ref[0] = (o * lax.rsqrt(ms + NORM_EPS) * g_ref[...]).astype(BF16)


def _diff_attn_call(dq, dk, dv, lam, gain, tq, tk):
    b, s, _ = dq.shape
    n_pairs = DIFF_QK // LANES
    grid = (b, n_pairs, s // tq)
    return pl.pallas_call(
        functools.partial(_diff_attn_body, tq, tk),
        grid=grid,
        in_specs=[pl.BlockSpec(memory_space=pltpu.SMEM),
                  pl.BlockSpec((1, tq, LANES), lambda bi, hp, i: (bi, i, hp)),
                  pl.BlockSpec((1, s, LANES), lambda bi, hp, i: (bi, 0, hp)),
                  pl.BlockSpec((1, s, LANES), lambda bi, hp, i: (bi, 0, hp)),
                  pl.BlockSpec((1, LANES), lambda bi, hp, i: (0, hp))],
        out_specs=pl.BlockSpec((1, tq, LANES), lambda bi, hp, i: (bi, i, hp)),
        out_shape=jax.ShapeDtypeStruct((b, s, DIFF_W), BF16),
        scratch_shapes=[pltpu.VMEM((4 * tq, LANES), F32)] * 3 + [pltpu.VMEM((4 * tq, tk), F32)] * 2,
        compiler_params=_cparams(3), name="diff_attn",
    )(lam, dq, dk, dv, gain)


def _mla_attn_body(tq, tk, q_ref, k_ref, v_ref, o_ref, m_ref, l_ref, acc_ref, s_a, s_b):
    i = pl.program_id(2)
    q = q_ref[0]
    _init_state(m_ref, l_ref, acc_ref)

    def scores(j, s_ref):
        k = k_ref[0, pl.ds(pl.multiple_of(j * tk, tk), tk), :]
        for hh in range(2):
            sl = slice(hh * MLA_HEAD_PAD, (hh + 1) * MLA_HEAD_PAD)
            s_ref[hh] = _dot_nt(q[:, sl], k[:, sl])

    def update(s_ref, j, masked):
        v = v_ref[0, pl.ds(pl.multiple_of(j * tk, tk), tk), :]
        mask = _causal_mask(tq, tq, tk, i, j) if masked else None
        for hh in range(2):
            _flash_update(s_ref[hh], v, mask, m_ref.at[hh], l_ref.at[hh], acc_ref.at[hh])

    _causal_sweep(tq, tk, i, scores, update, s_a, s_b)
    lane = lax.broadcasted_iota(I32, (1, LANES), 1)
    o = acc_ref[...] / jnp.sum(l_ref[...], axis=-1, keepdims=True)
    o_ref[0] = jnp.where(lane < MLA_V, o[0], o[1]).astype(BF16)


def _mla_attn_call(qm, km, vm, tq, tk):
    b, s, _ = qm.shape
    n_pairs = MLA_HEADS // 2
    grid = (b, n_pairs, s // tq)
    qw = 2 * MLA_HEAD_PAD
    return pl.pallas_call(
        functools.partial(_mla_attn_body, tq, tk),
        grid=grid,
        in_specs=[pl.BlockSpec((1, tq, qw), lambda bi, hp, i: (bi, i, hp)),
                  pl.BlockSpec((1, s, qw), lambda bi, hp, i: (bi, 0, hp)),
                  pl.BlockSpec((1, s, LANES), lambda bi, hp, i: (bi, 0, hp))],
        out_specs=pl.BlockSpec((1, tq, LANES), lambda bi, hp, i: (bi, i, hp)),
        out_shape=jax.ShapeDtypeStruct((b, s, MLA_W), BF16),
        scratch_shapes=[pltpu.VMEM((2, tq, LANES), F32)] * 3 + [pltpu.VMEM((2, tq, tk), F32)] * 2,
        compiler_params=_cparams(3), name="mla_attn",
    )(qm, km, vm)


def _decode_body(layer, sps, pps, n_pages, *refs):
    pt_ref = refs[0]
    (dq_ref, qm_ref, dks_ref, dvs_ref, ckvs_ref, kpes_ref, wk_ref, sel_ref, wuv_ref,
     gain_ref, lam_ref) = refs[1:12]
    cache_refs = refs[12:16]
    do_ref, mo_ref = refs[16:18]
    (qbd_ref, qlat_ref, qpe_ref, md_ref, ld_ref, accd_ref, mm_ref, lm_ref, accm_ref) = refs[18:27]
    bufs = refs[27:31]
    sem = refs[31]

    bi = pl.program_id(0)
    p = pl.program_id(1)
    n_steps = pl.num_programs(1)
    g = bi * n_steps + p
    n_rows = SUBLANES

    def page_copies(step_bi, step_p, slot, lookup):
        out = []
        for u in range(sps):
            for r in range(pps):
                page = pt_ref[(step_bi * sps + u) * n_pages + step_p * pps + r] if lookup else 0
                for c in range(4):
                    out.append(pltpu.make_async_copy(cache_refs[c].at[layer, page], bufs[c].at[slot, u * pps + r],
                                                     sem.at[slot, c]))
        return out

    @pl.when(g == 0)
    def _():
        for cp in page_copies(bi, p, 0, True):
            cp.start()

    @pl.when(g + 1 < pl.num_programs(0) * n_steps)
    def _():
        wrap = p == n_steps - 1
        for cp in page_copies(jnp.where(wrap, bi + 1, bi), jnp.where(wrap, 0, p + 1), (g + 1) & 1, True):
            cp.start()

    slot = g & 1
    for cp in page_copies(bi, p, slot, False):
        cp.wait()

    @pl.when(p == 0)
    def _():
        for u in range(sps):
            sub = lax.broadcasted_iota(I32, (n_rows, DIFF_QK), 0)
            lane = lax.broadcasted_iota(I32, (n_rows, DIFF_QK), 1)
            row = jnp.broadcast_to(dq_ref[u].astype(F32), (n_rows, DIFF_QK))
            qbd = jnp.where((lane >> 5) == sub, row, 0.0)
            qbd_ref[u] = qbd.astype(BF16)
            md_ref[u] = jnp.sum(qbd * dks_ref[u], axis=1, keepdims=True)
            ld_ref[u] = jnp.ones((n_rows, 1), F32)
            accd_ref[u] = jnp.broadcast_to(dvs_ref[u], (n_rows, DIFF_W))

            wq = MLA_HEADS * MLA_HEAD_PAD
            sub = lax.broadcasted_iota(I32, (n_rows, wq), 0)
            lane = lax.broadcasted_iota(I32, (n_rows, wq), 1)
            row = jnp.broadcast_to(qm_ref[u].astype(F32), (n_rows, wq))
            qf = jnp.where((lane >> 7) == sub, row, 0.0).astype(BF16)
            qlat = _dot_nt(qf, wk_ref[...]).astype(BF16)
            qpe = _dot(qf, sel_ref[...]).astype(BF16)
            qlat_ref[u] = qlat
            qpe_ref[u] = qpe
            mm_ref[u] = (jnp.sum(qlat.astype(F32) * ckvs_ref[u], axis=1, keepdims=True)
                         + jnp.sum(qpe.astype(F32) * kpes_ref[u], axis=1, keepdims=True))
            lm_ref[u] = jnp.ones((n_rows, 1), F32)
            accm_ref[u] = jnp.broadcast_to(ckvs_ref[u], (n_rows, MLA_KV))

    def rows(c, u):
        x = bufs[c][slot, u * pps:(u + 1) * pps]
        return x.reshape(pps * PAGE, x.shape[-1]).astype(BF16)

    def update(s, v, u, m_ref, l_ref, acc_ref):
        m_prev = m_ref[u]
        m_new = jnp.maximum(m_prev, jnp.max(s, axis=-1, keepdims=True))
        alpha = jnp.exp(m_prev - m_new)
        pr = jnp.exp(s - m_new)
        l_ref[u] = alpha * l_ref[u] + jnp.sum(pr, axis=-1, keepdims=True)
        acc_ref[u] = alpha * acc_ref[u] + _dot(pr.astype(BF16), v)
        m_ref[u] = m_new

    for u in range(sps):
        update(_dot_nt(qbd_ref[u], rows(0, u)), rows(1, u), u, md_ref, ld_ref, accd_ref)
        c = rows(2, u)
        kpe_t = jnp.concatenate([bufs[3][slot, u * pps + r] for r in range(pps)], axis=1).astype(BF16)
        update(_dot_nt(qlat_ref[u], c) + _dot(qpe_ref[u], kpe_t), c, u, mm_ref, lm_ref, accm_ref)

    @pl.when(p == n_steps - 1)
    def _():
        lam = lam_ref[...]
        for u in range(sps):
            sub = lax.broadcasted_iota(I32, (n_rows, DIFF_W), 0)
            lane = lax.broadcasted_iota(I32, (n_rows, DIFF_W), 1)
            coef = jnp.where((sub & 1) == 0, 1.0, -lam)
            o2 = accd_ref[u] / ld_ref[u] * coef
            orow = jnp.sum(jnp.where((lane >> 6) == (sub >> 1), o2, 0.0), axis=0, keepdims=True)
            lane1 = lane[0:1, :] >> 6
            sq = orow * orow
            ms = jnp.zeros_like(orow)
            for g in range(DIFF_HEADS):
                ssg = jnp.sum(jnp.where(lane1 == g, sq, 0.0), axis=-1, keepdims=True)
                ms = jnp.where(lane1 == g, ssg, ms)
            ms = ms * (1.0 / DIFF_V)
            do_ref[u] = (orow * lax.rsqrt(ms + NORM_EPS) * gain_ref[...]).astype(BF16)

            olat = (accm_ref[u] / lm_ref[u]).astype(BF16)
            r = _dot(olat, wuv_ref[...])
            sub = lax.broadcasted_iota(I32, (n_rows, MLA_W), 0)
            lane = lax.broadcasted_iota(I32, (n_rows, MLA_W), 1)
            mo_ref[u] = jnp.sum(jnp.where((lane >> 6) == sub, r, 0.0), axis=0, keepdims=True).astype(BF16)


def _decode_call(layer, page_table, caches, dq, qm, dks, dvs, ckvs, kpes, lw, sps, pps):
    ns, n_pages = page_table.shape
    assert n_pages % pps == 0 and ns % sps == 0
    n_steps = n_pages // pps
    cache_k, cache_v, cache_ckv, cache_kpe = caches
    cache_kpe_t = jnp.swapaxes(cache_kpe, 2, 3)

    def tok(w):
        return pl.BlockSpec((sps, 1, w), lambda bi, p, pt: (bi, 0, 0))

    def full(shape):
        nd = len(shape)
        return pl.BlockSpec(shape, lambda bi, p, pt: (0,) * nd)

    weights = [lw["wk"], lw["sel"], lw["wv"], lw["diff_gain"], lw["lam11"]]
    n_pg = sps * pps
    cache_args = [cache_k, cache_v, cache_ckv, cache_kpe_t]
    in_specs = ([tok(DIFF_QK), tok(MLA_HEADS * MLA_HEAD_PAD), tok(DIFF_QK), tok(DIFF_W), tok(MLA_KV), tok(MLA_ROPE)]
                + [full(w.shape) for w in weights]
                + [pl.BlockSpec(memory_space=pl.ANY)] * len(cache_args))
    args = [dq, qm, dks, dvs, ckvs, kpes] + weights + cache_args

    def state(w, dt=F32):
        return pltpu.VMEM((sps, SUBLANES, w), dt)

    n_slots = 2
    page_bufs = [pltpu.VMEM((n_slots, n_pg) + c.shape[2:], c.dtype) for c in cache_args]
    grid_spec = pltpu.PrefetchScalarGridSpec(
        num_scalar_prefetch=1, grid=(ns // sps, n_steps), in_specs=in_specs,
        out_specs=[tok(DIFF_W), tok(MLA_W)],
        scratch_shapes=[state(DIFF_QK, BF16), state(MLA_KV, BF16), state(MLA_ROPE, BF16),
                        state(1), state(1), state(DIFF_W), state(1), state(1), state(MLA_KV)]
        + page_bufs + [pltpu.SemaphoreType.DMA((n_slots, len(cache_args)))])
    return pl.pallas_call(
        functools.partial(_decode_body, layer, sps, pps, n_pages),
        grid_spec=grid_spec,
        out_shape=[jax.ShapeDtypeStruct((ns, 1, DIFF_W), BF16), jax.ShapeDtypeStruct((ns, 1, MLA_W), BF16)],
        compiler_params=_cparams(2), name="decode_attn",
    )(page_table.reshape(-1), *args)


def _post_body(x_ref, y_ref, d_ref, m_ref, wo_ref, g2_ref, wqh_ref, wql_ref, x1_ref, h2_ref, q_ref):
    x1 = (x_ref[...] + _dot(y_ref[...], wo_ref[0:CONV_W, :])
          + _dot(d_ref[...], wo_ref[CONV_W:CONV_W + DIFF_W, :])
          + _dot(m_ref[...], wo_ref[CONV_W + DIFF_W:, :]))
    x1_ref[...] = x1
    h2 = _rms(x1, g2_ref[...])
    hb = h2.astype(BF16)
    hl = (h2 - hb.astype(F32)).astype(BF16)
    q_ref[...] = _dot(hb, wqh_ref[...]) + _dot(hb, wql_ref[...]) + _dot(hl, wqh_ref[...])
    h2_ref[...] = hb


def _post_call(x, y, d, m, lw):
    t = x.shape[0]
    tb = min(t, 512)
    qw = PEER_HEADS * PEER_DKEY

    def tok(w):
        return pl.BlockSpec((tb, w), lambda i: (i, 0))

    weights = [lw["w_out"], lw["g2"], lw["wq_hi"], lw["wq_lo"]]
    return pl.pallas_call(
        _post_body, grid=(t // tb,),
        in_specs=[tok(D_MODEL), tok(CONV_W), tok(DIFF_W), tok(MLA_W)] + [_full_spec(w.shape) for w in weights],
        out_specs=[tok(D_MODEL), tok(D_MODEL), tok(qw)],
        out_shape=[jax.ShapeDtypeStruct((t, D_MODEL), F32), jax.ShapeDtypeStruct((t, D_MODEL), BF16),
                   jax.ShapeDtypeStruct((t, qw), F32)],
        compiler_params=_cparams(1), name="post_attn",
    )(x, y, d, m, *weights)


_CAND_PAIRS = [(a, b_) for a in range(PEER_TOPK) for b_ in range(PEER_TOPK) if (a + 1) * (b_ + 1) <= PEER_TOPK]
_CAND_ROWS = -(-len(_CAND_PAIRS) // SUBLANES) * SUBLANES


def _cand_tables():
    n_pad = _CAND_ROWS - len(_CAND_PAIRS)
    flat = [a * PEER_TOPK + b_ for a, b_ in _CAND_PAIRS] + [PEER_TOPK * PEER_TOPK] * n_pad
    pen = [0.0] * len(_CAND_PAIRS) + [-math.inf] * n_pad
    rows = jnp.arange(PEER_TOPK)[None, :]
    pick_a = (jnp.asarray([a for a, _ in _CAND_PAIRS] + [0] * n_pad)[:, None] == rows).astype(BF16)
    pick_b = (jnp.asarray([b_ for _, b_ in _CAND_PAIRS] + [0] * n_pad)[:, None] == rows).astype(BF16)
    return (jnp.asarray(flat, I32).reshape(-1, 1), jnp.asarray(pen, F32).reshape(-1, 1), pick_a, pick_b)


def _pick_rows(onehot, x):
    hi = x.astype(BF16)
    r1 = x - hi.astype(F32)
    mid = r1.astype(BF16)
    lo = (r1 - mid.astype(F32)).astype(BF16)
    return (_dot(onehot, hi) + _dot(onehot, mid)) + _dot(onehot, lo)


_ROUTE_HEADS = 4


def _route_body(tb, q_ref, khi_ref, klo_ref, flat_ref, pen_ref, pa_ref, pb_ref, io_ref, jo_ref, go_ref,
                sv_ref, si_ref, i_scr, j_scr, g_scr):
    hg = pl.program_id(1)
    q = q_ref[...]
    qh = q.astype(BF16)
    ql = (q - qh.astype(F32)).astype(BF16)
    n_iota = lax.broadcasted_iota(I32, (PEER_KEYS, tb), 0)

    for u in range(_ROUTE_HEADS):
        def sub_scores(p, u=u):
            c0 = (2 * u + p) * PEER_KEYS
            kh, kl = khi_ref[u, p], klo_ref[u, p]
            qh_, ql_ = qh[:, c0:c0 + PEER_KEYS], ql[:, c0:c0 + PEER_KEYS]
            return _dot_nt(kh, qh_) + _dot_nt(kh, ql_) + _dot_nt(kl, qh_)

        def it(k, ss, u=u):
            out = []
            for p, s in enumerate(ss):
                m = jnp.max(s, axis=0, keepdims=True)
                idx = jnp.min(jnp.where(s == m, n_iota, PEER_KEYS), axis=0, keepdims=True)
                sv_ref[u, p, pl.ds(k, 1), :] = m
                si_ref[u, p, pl.ds(k, 1), :] = idx
                out.append(jnp.where(n_iota == idx, -jnp.inf, s))
            return tuple(out)

        lax.fori_loop(0, PEER_TOPK, it, (sub_scores(0), sub_scores(1)))

    cands = tuple((_pick_rows(pa_ref[...], sv_ref[u, 0]) + _pick_rows(pb_ref[...], sv_ref[u, 1])) + pen_ref[...]
                  for u in range(_ROUTE_HEADS))
    flat = jnp.broadcast_to(flat_ref[...], (_CAND_ROWS, tb))
    k_iota = lax.broadcasted_iota(I32, (PEER_TOPK, tb), 0)
    base = pl.multiple_of(hg * (_ROUTE_HEADS * PEER_TOPK), _ROUTE_HEADS * PEER_TOPK)

    def it2(k, cs):
        out = []
        for u, cand in enumerate(cs):
            row = base + u * PEER_TOPK + k
            m = jnp.max(cand, axis=0, keepdims=True)
            fl = jnp.min(jnp.where(cand == m, flat, PEER_TOPK * PEER_TOPK), axis=0, keepdims=True)
            a = fl >> 4
            b_ = fl & (PEER_TOPK - 1)
            g_scr[pl.ds(row, 1), :] = m
            i_scr[pl.ds(row, 1), :] = jnp.sum(jnp.where(k_iota == a, si_ref[u, 0], 0), axis=0, keepdims=True)
            j_scr[pl.ds(row, 1), :] = jnp.sum(jnp.where(k_iota == b_, si_ref[u, 1], 0), axis=0, keepdims=True)
            out.append(jnp.where(flat == fl, -jnp.inf, cand))
        return tuple(out)

    lax.fori_loop(0, PEER_TOPK, it2, cands)
    for u in range(_ROUTE_HEADS):
        rows = pl.ds(base + u * PEER_TOPK, PEER_TOPK)
        ts = g_scr[rows, :]
        e = jnp.exp(ts - ts[0:1, :])
        g_scr[rows, :] = e / jnp.sum(e, axis=0, keepdims=True)

    @pl.when(hg == pl.num_programs(1) - 1)
    def _():
        io_ref[...] = i_scr[...].T
        jo_ref[...] = j_scr[...].T
        go_ref[...] = g_scr[...].T


def _route_call(q, lw):
    t = q.shape[0]
    tb = min(t, 256)
    tables = _cand_tables()
    tokq = pl.BlockSpec((tb, _ROUTE_HEADS * PEER_DKEY), lambda i, h: (i, h))
    keys = pl.BlockSpec((_ROUTE_HEADS, 2, PEER_KEYS, PEER_DKEY // 2), lambda i, h: (h, 0, 0, 0))
    out = pl.BlockSpec((tb, PEER_SEL), lambda i, h: (i, 0))
    return pl.pallas_call(
        functools.partial(_route_body, tb), grid=(t // tb, PEER_HEADS // _ROUTE_HEADS),
        in_specs=[tokq, keys, keys] + [pl.BlockSpec(tb_.shape, lambda i, h: (0, 0)) for tb_ in tables],
        out_specs=[out, out, out],
        out_shape=[jax.ShapeDtypeStruct((t, PEER_SEL), I32), jax.ShapeDtypeStruct((t, PEER_SEL), I32),
                   jax.ShapeDtypeStruct((t, PEER_SEL), F32)],
        scratch_shapes=[pltpu.VMEM((_ROUTE_HEADS, 2, PEER_TOPK, tb), F32),
                        pltpu.VMEM((_ROUTE_HEADS, 2, PEER_TOPK, tb), I32),
                        pltpu.VMEM((PEER_SEL, tb), I32), pltpu.VMEM((PEER_SEL, tb), I32),
                        pltpu.VMEM((PEER_SEL, tb), F32)],
        compiler_params=_cparams(2), name="peer_route",
    )(q, lw["keys_hi"], lw["keys_lo"], *tables)


_CHUNK_ROWS = 32
_CHUNK = _CHUNK_ROWS * PEER_KEYS


def _up_body(h_ref, u_ref, i_ref, j_ref, g_ref, o_ref, val_ref):
    c = pl.program_id(1)

    @pl.when(c == 0)
    def _():
        val_ref[...] = jnp.zeros(val_ref.shape, F32)

    a = _dot_nt(h_ref[...], u_ref[...])
    iv, jv = i_ref[...], j_ref[...]
    acc = val_ref[...]
    for r in range(_CHUNK_ROWS):
        got = jnp.take_along_axis(a[:, r * PEER_KEYS:(r + 1) * PEER_KEYS], jv, axis=1)
        acc = acc + jnp.where(iv == c * _CHUNK_ROWS + r, got, 0.0)
    val_ref[...] = acc

    @pl.when(c == pl.num_programs(1) - 1)
    def _():
        o_ref[...] = g_ref[...] * (0.5 * acc * (1.0 + lax.erf(acc * (2.0 ** -0.5))))


def _up_call(h2, i_idx, j_idx, gate, lw):
    t = h2.shape[0]
    tb = min(t, 512)
    sel = pl.BlockSpec((tb, PEER_SEL), lambda i, c: (i, 0))
    return pl.pallas_call(
        _up_body, grid=(t // tb, N_EXPERTS // _CHUNK),
        in_specs=[pl.BlockSpec((tb, D_MODEL), lambda i, c: (i, 0)),
                  pl.BlockSpec((_CHUNK, D_MODEL), lambda i, c: (c, 0)), sel, sel, sel],
        out_specs=sel,
        out_shape=jax.ShapeDtypeStruct((t, PEER_SEL), F32),
        scratch_shapes=[pltpu.VMEM((tb, PEER_SEL), F32)],
        compiler_params=_cparams(2), name="peer_up",
    )(h2, lw["u"], i_idx, j_idx, gate)


_Y_PITCH = PEER_KEYS + SUBLANES


def _down_body(final, tb, x_ref, w_ref, i_ref, j_ref, v_ref, gf_ref, o_ref, y_scr, acc_ref):
    c = pl.program_id(1)

    @pl.when(c == 0)
    def _():
        acc_ref[...] = jnp.zeros(acc_ref.shape, F32)
        shape = (PEER_KEYS, PEER_SEL)
        sub = lax.broadcasted_iota(I32, shape, 0)

        def token(t, carry):
            wrow = jnp.broadcast_to(w_ref[pl.ds(t, 1), :], shape)
            irow = jnp.broadcast_to(i_ref[pl.ds(t, 1), :], shape)
            jrow = jnp.broadcast_to(j_ref[pl.ds(t, 1), :], shape)
            lhs = jnp.where(irow == sub, wrow, 0.0).astype(BF16)
            rhs = jnp.where(jrow == sub, 1.0, 0.0).astype(BF16)
            y_scr[pl.ds(pl.multiple_of(t * _Y_PITCH, SUBLANES), PEER_KEYS), :] = _dot_nt(lhs, rhs)
            return carry

        lax.fori_loop(0, tb, token, 0, unroll=64)

    base = c * _CHUNK_ROWS
    lhs = jnp.concatenate([y_scr[pl.ds(base + r, tb, stride=_Y_PITCH), :] for r in range(_CHUNK_ROWS)],
                          axis=1).astype(BF16)
    acc_ref[...] += _dot(lhs, v_ref[...])

    @pl.when(c == pl.num_programs(1) - 1)
    def _():
        xn = x_ref[...] + acc_ref[...]
        if final:
            xn = _rms(xn, gf_ref[...])
        o_ref[...] = xn


def _down_call(x1, wgt, i_idx, j_idx, lw, gfin, final):
    t = x1.shape[0]
    tb = min(t, 256)
    sel = pl.BlockSpec((tb, PEER_SEL), lambda i, c: (i, 0))
    tok = pl.BlockSpec((tb, D_MODEL), lambda i, c: (i, 0))
    return pl.pallas_call(
        functools.partial(_down_body, final, tb), grid=(t // tb, N_EXPERTS // _CHUNK),
        in_specs=[tok, sel, sel, sel, pl.BlockSpec((_CHUNK, D_MODEL), lambda i, c: (c, 0)),
                  pl.BlockSpec((1, D_MODEL), lambda i, c: (0, 0))],
        out_specs=tok,
        out_shape=jax.ShapeDtypeStruct((t, D_MODEL), F32),
        scratch_shapes=[pltpu.VMEM((tb * _Y_PITCH, PEER_KEYS), F32), pltpu.VMEM((tb, D_MODEL), F32)],
        compiler_params=_cparams(2), name="peer_down",
    )(x1, wgt, i_idx, j_idx, lw["v"], gfin)


def _split_bf16(w):
    hi = w.astype(BF16)
    return hi, (w - hi.astype(F32)).astype(BF16)


def _rot_half(w):
    half = MLA_ROPE // 2
    return jnp.concatenate([-w[..., half:], w[..., :half]], axis=-1)


def _prep_layer(l, p):
    w_in = p["w_in"][l]
    kpe_w = w_in[:, IN_W - MLA_ROPE:]
    w_in_ext = jnp.concatenate(
        [w_in[:, :IN_W - MLA_ROPE], jnp.zeros((D_MODEL, MLA_NOPE), F32), kpe_w, _rot_half(kpe_w)], axis=1)
    w_uq = p["mla_w_uq"][l]
    pad = jnp.zeros((MLA_Q, MLA_HEADS, MLA_HEAD_PAD - MLA_NOPE - MLA_ROPE), F32)
    wq = jnp.concatenate([w_uq, pad], axis=-1).reshape(MLA_Q, -1)
    wqr = jnp.concatenate([jnp.zeros((MLA_Q, MLA_HEADS, MLA_NOPE), F32), _rot_half(w_uq[..., MLA_NOPE:]), pad],
                          axis=-1).reshape(MLA_Q, -1)
    w_uk = p["mla_w_uk"][l]
    wk = jnp.concatenate([w_uk, jnp.zeros((MLA_KV, MLA_HEADS, MLA_HEAD_PAD - MLA_NOPE), F32)],
                         axis=-1).reshape(MLA_KV, -1)
    rows = jnp.arange(MLA_HEADS * MLA_HEAD_PAD)
    sel = ((rows[:, None] % MLA_HEAD_PAD) == (MLA_NOPE + jnp.arange(MLA_ROPE))[None, :]).astype(BF16)
    lp = p["diff_lambda"][l].astype(F32)
    lam_init = 0.8 - 0.6 * math.exp(-0.3 * l)
    lam = jnp.exp(jnp.sum(lp[0] * lp[1])) - jnp.exp(jnp.sum(lp[2] * lp[3])) + lam_init
    wq_hi, wq_lo = _split_bf16(p["peer_w_q"][l])
    keys_hi, keys_lo = _split_bf16(p["peer_sub_keys"][l])
    return {
        "g1": p["ln1_g"][l].reshape(1, -1), "g2": p["ln2_g"][l].reshape(1, -1),
        "w_in": w_in_ext.astype(BF16), "conv_w": p["conv_w"][l],
        "qg": p["mla_q_norm_g"][l].reshape(1, -1), "kvg": p["mla_kv_norm_g"][l].reshape(1, -1),
        "wq": wq.astype(BF16), "wqr": wqr.astype(BF16), "wk": wk.astype(BF16),
        "wv": p["mla_w_uv"][l].reshape(MLA_KV, MLA_W).astype(BF16), "sel": sel,
        "lam": lam.reshape(1), "lam11": lam.reshape(1, 1),
        "diff_gain": (jnp.tile(p["diff_norm_g"][l], DIFF_HEADS) * (1.0 - lam_init)).reshape(1, -1),
        "w_out": p["w_out"][l].astype(BF16), "wq_hi": wq_hi, "wq_lo": wq_lo,
        "keys_hi": keys_hi, "keys_lo": keys_lo,
        "u": p["peer_u"][l].astype(BF16), "v": p["peer_v"][l].astype(BF16),
    }


def _rope_tabs(pos):
    half = MLA_ROPE // 2
    inv = ROPE_THETA ** (-jnp.arange(half, dtype=F32) * (2.0 / MLA_ROPE))
    ang = pos.astype(F32)[:, None] * inv[None, :]
    cos2 = jnp.concatenate([jnp.cos(ang)] * 2, axis=-1)
    sin2 = jnp.concatenate([jnp.sin(ang)] * 2, axis=-1)
    n = pos.shape[0]
    z64 = jnp.zeros((n, MLA_NOPE), F32)
    z32 = jnp.zeros((n, LANES - MLA_NOPE - MLA_ROPE), F32)
    cq = jnp.concatenate([jnp.ones((n, MLA_NOPE), F32), cos2, z32], axis=1)
    ck = jnp.concatenate([z64, cos2, z32], axis=1)
    sn = jnp.concatenate([z64, sin2, z32], axis=1)
    return cq, ck, sn


def _peer_and_residual(x1, h2, q, lw, gfin, final):
    i_idx, j_idx, gate = _route_call(q, lw)
    wgt = _up_call(h2, i_idx, j_idx, gate, lw)
    return _down_call(x1, wgt, i_idx, j_idx, lw, gfin, final)


def _layer_prompt(x, lw, tabs, gfin, final, tq_diff=512, tq_mla=512, tk=512):
    b, s, _ = x.shape
    tabs3 = [t[None] for t in tabs]
    (y, dqb, dk, dkb, dv, dvb, ckv, kpe, qm, km, vm, newconv) = _proj_call(x, lw, tabs3)
    tk = min(tk, s)
    d_out = _diff_attn_call(dqb, dkb, dvb, lw["lam"], lw["diff_gain"], min(tq_diff, s), tk)
    m_out = _mla_attn_call(qm, km, vm, min(tq_mla, s), tk)
    t = b * s
    x1, h2, q = _post_call(x.reshape(t, -1), y.reshape(t, -1), d_out.reshape(t, -1), m_out.reshape(t, -1), lw)
    xn = _peer_and_residual(x1, h2, q, lw, gfin, final)
    return xn.reshape(b, s, -1), (newconv, dk, dv, ckv, kpe)


def _layer_sample(x, lw, tabs, gfin, final, layer, state, caches, page_table, sps=2, pps=16):
    ns = x.shape[0]
    x3 = x.reshape(1, ns, -1)
    tabs3 = [t[None] for t in tabs]
    st = (state[:, 0, :][None], state[:, 1, :][None])
    (y, dqb, dk, dkb, dv, dvb, ckv, kpe, qm, km, vm, z) = _proj_call(x3, lw, tabs3, st)
    del dkb, dvb, km, vm

    def per_tok(a):
        return a.reshape(ns, 1, -1)

    d_out, m_out = _decode_call(layer, page_table, caches, per_tok(dqb), per_tok(qm), per_tok(dk), per_tok(dv),
                                per_tok(ckv), per_tok(kpe), lw, sps, min(pps, page_table.shape[1]))
    x1, h2, q = _post_call(x.reshape(ns, -1), y.reshape(ns, -1), d_out.reshape(ns, -1), m_out.reshape(ns, -1), lw)
    xn = _peer_and_residual(x1, h2, q, lw, gfin, final)
    newconv = jnp.stack([state[:, 1, :], z.reshape(ns, -1)], axis=1)
    return xn.reshape(ns, 1, -1), (newconv, per_tok(dk), per_tok(dv), per_tok(ckv), per_tok(kpe))


def kernel(x_prompt, x_sample, state_conv, cache_diff_k, cache_diff_v, cache_mla_ckv, cache_mla_kpe,
           page_table, ln1_g, ln2_g, w_in, conv_w, diff_lambda, diff_norm_g, mla_q_norm_g, mla_kv_norm_g,
           mla_w_uq, mla_w_uk, mla_w_uv, w_out, peer_w_q, peer_sub_keys, peer_u, peer_v, final_norm_g):
    params = dict(ln1_g=ln1_g, ln2_g=ln2_g, w_in=w_in, conv_w=conv_w, diff_lambda=diff_lambda,
                  diff_norm_g=diff_norm_g, mla_q_norm_g=mla_q_norm_g, mla_kv_norm_g=mla_kv_norm_g,
                  mla_w_uq=mla_w_uq, mla_w_uk=mla_w_uk, mla_w_uv=mla_w_uv, w_out=w_out,
                  peer_w_q=peer_w_q, peer_sub_keys=peer_sub_keys, peer_u=peer_u, peer_v=peer_v)
    depth = ln1_g.shape[0]
    s = x_prompt.shape[1]
    ns = x_sample.shape[0]
    past_len = page_table.shape[1] * cache_diff_k.shape[2]
    tabs_p = _rope_tabs(jnp.arange(s))
    tabs_s = _rope_tabs(jnp.full((ns,), past_len, I32))
    gfin = final_norm_g.reshape(1, -1)
    caches = (cache_diff_k, cache_diff_v, cache_mla_ckv, cache_mla_kpe)

    xp, xs = x_prompt, x_sample.reshape(ns, -1)
    rows_p, rows_s = [], []
    for l in range(depth):
        lw = _prep_layer(l, params)
        final = l == depth - 1
        xp, new_p = _layer_prompt(xp, lw, tabs_p, gfin, final)
        xs3, new_s = _layer_sample(xs, lw, tabs_s, gfin, final, l, state_conv[l], caches, page_table)
        xs = xs3.reshape(ns, -1)
        rows_p.append(new_p)
        rows_s.append(new_s)
    outs_p = [jnp.stack(r) for r in zip(*rows_p)]
    outs_s = [jnp.stack(r) for r in zip(*rows_s)]
    return (xp, xs.reshape(ns, 1, -1), *outs_p, *outs_s)
```
